```python
import jax, jax.numpy as jnp
from jax import lax
import numpy as np

D_MODEL = 1024
BATCH = 8
SEQ = 4096
DEPTH = 2

GRID_W = 64
CTX_LEN = 256
EPS = 1e-6
HEAD_DIM = 64
ATTN_W = D_MODEL // 2
N_Q_HEADS = ATTN_W // HEAD_DIM
N_KV_HEADS = N_Q_HEADS // 4
Q_GROUP = N_Q_HEADS // N_KV_HEADS
KV_W = N_KV_HEADS * HEAD_DIM
Q_BLOCK = 128
ROPE_THETA = 10000.0
AXIS_DIM = HEAD_DIM // 2
CONV_W = D_MODEL // 4
CONV_K = 31
CHUNK_W = D_MODEL // 4
CHUNK_HEADS = 4
CHUNK_HEAD_DIM = CHUNK_W // CHUNK_HEADS
CHUNK = 128
MIX_W = ATTN_W + CONV_W + CHUNK_W
Q0 = 0
K0 = Q0 + ATTN_W
V0 = K0 + KV_W
CV0 = V0 + KV_W
CH0 = CV0 + 2 * CONV_W
IN_W = CH0 + 2 * CHUNK_W
N_EXPERTS = 32
TOP_K = 4
D_EXPERT = D_MODEL
SWIGLU_LIMIT = 7.0
SWIGLU_ALPHA = 1.702
EXPERT_BLOCK = 128

kernel_name = "hybrid_parallel_groups_moe_dit"


def rmsnorm(x, g):
    xf = x.astype(jnp.float32)
    y = xf * lax.rsqrt(jnp.mean(xf * xf, axis=-1, keepdims=True) + EPS)
    return (y * g.astype(jnp.float32)).astype(x.dtype)


def layernorm(x, g, b):
    xf = x.astype(jnp.float32)
    mu = jnp.mean(xf, axis=-1, keepdims=True)
    xc = xf - mu
    y = xc * lax.rsqrt(jnp.mean(xc * xc, axis=-1, keepdims=True) + EPS)
    return (y * g.astype(jnp.float32) + b.astype(jnp.float32)).astype(x.dtype)


def axial_rope_tables(n_tokens):
    rows = n_tokens // GRID_W
    r, col = jnp.meshgrid(jnp.arange(rows), jnp.arange(GRID_W), indexing="ij")
    inv = ROPE_THETA ** (-jnp.arange(0, AXIS_DIM, 2, dtype=jnp.float32) / AXIS_DIM)
    ang = jnp.concatenate([r.reshape(-1, 1).astype(jnp.float32) * inv,
                           col.reshape(-1, 1).astype(jnp.float32) * inv], axis=-1)
    return jnp.cos(ang), jnp.sin(ang)


def apply_rope(t, cos, sin):
    tf = t.astype(jnp.float32)
    half = HEAD_DIM // 2
    t1, t2 = tf[..., :half], tf[..., half:]
    cs, sn = cos[None, :, None, :], sin[None, :, None, :]
    return jnp.concatenate([t1 * cs - t2 * sn, t2 * cs + t1 * sn], axis=-1).astype(t.dtype)


def attend(q, k, v):
    s = jnp.einsum("bqkgd,bskd->bkgqs", q, k, preferred_element_type=jnp.float32) * (HEAD_DIM ** -0.5)
    p = jax.nn.softmax(s, axis=-1).astype(v.dtype)
    return jnp.einsum("bkgqs,bskd->bqkgd", p, v)


def latent_attention(q, k_all, v_all):
    B, S = q.shape[0], q.shape[1]
    nb = S // Q_BLOCK
    qb = q.reshape(B, nb, Q_BLOCK, N_KV_HEADS, Q_GROUP, HEAD_DIM).transpose(1, 0, 2, 3, 4, 5)
    o = lax.map(lambda qi: attend(qi, k_all, v_all), qb)
    return o.transpose(1, 0, 2, 3, 4, 5).reshape(B, S, ATTN_W)


def conformer_conv(z, w_dw, b_dw, g_ln, b_ln):
    u = z[..., :CONV_W] * jax.nn.sigmoid(z[..., CONV_W:])
    u = lax.conv_general_dilated(u, w_dw[:, None, :], window_strides=(1,),
                                 padding=((CONV_K // 2, CONV_K // 2),),
                                 dimension_numbers=("NWC", "WIO", "NWC"),
                                 feature_group_count=CONV_W) + b_dw
    return jax.nn.silu(layernorm(u, g_ln, b_ln))


def chunk_spatial_gating(z, g_ln, b_ln, w_s, b_s):
    z = jax.nn.gelu(z, approximate=False)
    u, v = z[..., :CHUNK_W], z[..., CHUNK_W:]
    v = layernorm(v, g_ln, b_ln)
    B, L = v.shape[0], v.shape[1]
    vc = v.reshape(B, L // CHUNK, CHUNK, CHUNK_HEADS, CHUNK_HEAD_DIM)
    s = jnp.einsum("hpq,bnqhd->bnphd", w_s, vc) + b_s.T[None, None, :, :, None]
    return u * s.reshape(B, L, CHUNK_W)


def token_mixer(h_lat, h_ctx, w_in, g_q, g_k, w_dw, b_dw, g_cln, b_cln, g_sln, b_sln, w_s, b_s, w_o,
                cos, sin, ctx_out):
    B, S = h_lat.shape[0], h_lat.shape[1]
    C = h_ctx.shape[1]
    p = h_lat @ w_in
    q = apply_rope(rmsnorm(p[..., Q0:K0].reshape(B, S, N_Q_HEADS, HEAD_DIM), g_q), cos, sin)
    k = apply_rope(rmsnorm(p[..., K0:V0].reshape(B, S, N_KV_HEADS, HEAD_DIM), g_k), cos, sin)
    v = p[..., V0:CV0].reshape(B, S, N_KV_HEADS, HEAD_DIM)
    if ctx_out:
        pc = h_ctx @ w_in
        kvc = pc[..., K0:CV0]
    else:
        kvc = h_ctx @ w_in[:, K0:CV0]
    kc = rmsnorm(kvc[..., :KV_W].reshape(B, C, N_KV_HEADS, HEAD_DIM), g_k)
    vc = kvc[..., KV_W:].reshape(B, C, N_KV_HEADS, HEAD_DIM)
    k_all = jnp.concatenate([k, kc], axis=1)
    v_all = jnp.concatenate([v, vc], axis=1)
    a_lat = latent_attention(q, k_all, v_all)
    conv_lat = conformer_conv(p[..., CV0:CH0], w_dw, b_dw, g_cln, b_cln)
    chunk_lat = chunk_spatial_gating(p[..., CH0:IN_W], g_sln, b_sln, w_s, b_s)
    y_lat = jnp.concatenate([a_lat, conv_lat, chunk_lat], axis=-1) @ w_o
    if not ctx_out:
        return y_lat, None
    qc = rmsnorm(pc[..., Q0:K0].reshape(B, C, N_Q_HEADS, HEAD_DIM), g_q)
    a_ctx = attend(qc.reshape(B, C, N_KV_HEADS, Q_GROUP, HEAD_DIM), kc, vc).reshape(B, C, ATTN_W)
    conv_ctx = conformer_conv(pc[..., CV0:CH0], w_dw, b_dw, g_cln, b_cln)
    chunk_ctx = chunk_spatial_gating(pc[..., CH0:IN_W], g_sln, b_sln, w_s, b_s)
    y_ctx = jnp.concatenate([a_ctx, conv_ctx, chunk_ctx], axis=-1) @ w_o
    return y_lat, y_ctx


def moe_ffn(h, w_router, b_router, w_gate, b_gate, w_up, b_up, w_down, b_down):
    N, D = h.shape
    logits = (h @ w_router).astype(jnp.float32) + b_router.astype(jnp.float32)
    top_vals, top_idx = lax.top_k(logits, TOP_K)
    weights = jax.nn.softmax(top_vals, axis=-1)
    A = N * TOP_K
    flat_e = top_idx.reshape(-1)
    flat_tok = jnp.repeat(jnp.arange(N, dtype=jnp.int32), TOP_K)
    flat_w = weights.reshape(-1)
    order = jnp.argsort(flat_e)
    e_sorted = flat_e[order]
    counts = jnp.bincount(flat_e, length=N_EXPERTS)
    padded = (counts + EXPERT_BLOCK - 1) // EXPERT_BLOCK * EXPERT_BLOCK
    ends = jnp.cumsum(counts)
    pends = jnp.cumsum(padded)
    dest = (pends - padded)[e_sorted] + jnp.arange(A, dtype=jnp.int32) - (ends - counts)[e_sorted]
    n_blocks = -(-A // EXPERT_BLOCK) + N_EXPERTS
    P = n_blocks * EXPERT_BLOCK
    buf_tok = jnp.zeros((P,), jnp.int32).at[dest].set(flat_tok[order])
    buf_w = jnp.zeros((P,), jnp.float32).at[dest].set(flat_w[order])
    block_e = jnp.minimum(jnp.searchsorted(pends, jnp.arange(n_blocks, dtype=jnp.int32) * EXPERT_BLOCK,
                                           side="right"), N_EXPERTS - 1)

    def expert_block(args):
        tok, e = args
        xb = h[tok]
        g = jnp.minimum(xb @ w_gate[e] + b_gate[e], SWIGLU_LIMIT)
        u = jnp.clip(xb @ w_up[e] + b_up[e], -SWIGLU_LIMIT, SWIGLU_LIMIT)
        a = (u + 1.0) * (g * jax.nn.sigmoid(SWIGLU_ALPHA * g))
        return a @ w_down[e] + b_down[e]

    ys = lax.map(expert_block, (buf_tok.reshape(n_blocks, EXPERT_BLOCK), block_e))
    out = jnp.zeros((N, D), jnp.float32).at[buf_tok].add(ys.reshape(P, D).astype(jnp.float32) * buf_w[:, None])
    return out.astype(h.dtype)


def setup_inputs(seed: int = 0) -> dict:
    key = jax.random.key(seed)
    ks = jax.random.split(key, 40)
    f32 = jnp.float32

    def nrm(k, shape, scale):
        return jax.random.normal(k, shape, f32) * scale

    L, D, E, F = DEPTH, D_MODEL, N_EXPERTS, D_EXPERT
    return {
        "x": nrm(ks[0], (BATCH, SEQ, D), 1.0),
        "c": nrm(ks[1], (BATCH, D), 1.0),
        "ctx": nrm(ks[2], (BATCH, CTX_LEN, D), 1.0),
        "c_ctx": nrm(ks[3], (D,), 1.0),
        "w_ada": nrm(ks[4], (L, D, 6 * D), 0.5 * D ** -0.5),
        "b_ada": nrm(ks[5], (L, 6 * D), 0.02),
        "g_norm1": 1.0 + nrm(ks[6], (L, D), 0.02),
        "w_in": nrm(ks[7], (L, D, IN_W), D ** -0.5),
        "g_q": 1.0 + nrm(ks[8], (L, HEAD_DIM), 0.02),
        "g_k": 1.0 + nrm(ks[9], (L, HEAD_DIM), 0.02),
        "w_dw": nrm(ks[10], (L, CONV_K, CONV_W), CONV_K ** -0.5),
        "b_dw": nrm(ks[11], (L, CONV_W), 0.02),
        "g_conv_ln": 1.0 + nrm(ks[12], (L, CONV_W), 0.02),
        "b_conv_ln": nrm(ks[13], (L, CONV_W), 0.02),
        "g_sgu_ln": 1.0 + nrm(ks[14], (L, CHUNK_W), 0.02),
        "b_sgu_ln": nrm(ks[15], (L, CHUNK_W), 0.02),
        "w_s": nrm(ks[16], (L, CHUNK_HEADS, CHUNK, CHUNK), CHUNK ** -0.5),
        "b_s": nrm(ks[17], (L, CHUNK_HEADS, CHUNK), 0.02),
        "w_o": nrm(ks[18], (L, MIX_W, D), MIX_W ** -0.5),
        "g_norm2": 1.0 + nrm(ks[19], (L, D), 0.02),
        "w_router": nrm(ks[20], (L, D, E), D ** -0.5),
        "b_router": nrm(ks[21], (L, E), 0.01),
        "w_gate": nrm(ks[22], (L, E, D, F), D ** -0.5),
        "b_gate": nrm(ks[23], (L, E, F), 0.02),
        "w_up": nrm(ks[24], (L, E, D, F), D ** -0.5),
        "b_up": nrm(ks[25], (L, E, F), 0.02),
        "w_down": nrm(ks[26], (L, E, F, D), F ** -0.5),
        "b_down": nrm(ks[27], (L, E, D), 0.02),
        "g_final": 1.0 + nrm(ks[28], (D,), 0.02),
    }


def reference(x, c, ctx, c_ctx, w_ada, b_ada, g_norm1, w_in, g_q, g_k, w_dw, b_dw, g_conv_ln, b_conv_ln,
              g_sgu_ln, b_sgu_ln, w_s, b_s, w_o, g_norm2, w_router, b_router, w_gate, b_gate, w_up, b_up,
              w_down, b_down, g_final):
    B, S, D = x.shape
    C = ctx.shape[1]
    cos, sin = axial_rope_tables(S)
    xc = ctx
    sc_lat = jax.nn.silu(c)
    sc_ctx = jax.nn.silu(c_ctx)
    for l in range(DEPTH):
        last = l == DEPTH - 1
        mod_l = sc_lat @ w_ada[l] + b_ada[l]
        mod_c = sc_ctx @ w_ada[l] + b_ada[l]
        sh1, sc1, gt1, sh2, sc2, gt2 = jnp.split(mod_l[:, None, :], 6, axis=-1)
        csh1, csc1, cgt1, csh2, csc2, cgt2 = jnp.split(mod_c, 6, axis=-1)
        h_lat = rmsnorm(x, g_norm1[l]) * (1.0 + sc1) + sh1
        h_ctx = rmsnorm(xc, g_norm1[l]) * (1.0 + csc1) + csh1
        y_lat, y_ctx = token_mixer(h_lat, h_ctx, w_in[l], g_q[l], g_k[l], w_dw[l], b_dw[l],
                                   g_conv_ln[l], b_conv_ln[l], g_sgu_ln[l], b_sgu_ln[l], w_s[l], b_s[l],
                                   w_o[l], cos, sin, not last)
        x = x + gt1 * y_lat
        h2_lat = rmsnorm(x, g_norm2[l]) * (1.0 + sc2) + sh2
        if not last:
            xc = xc + cgt1 * y_ctx
            h2_ctx = rmsnorm(xc, g_norm2[l]) * (1.0 + csc2) + csh2
            tokens = jnp.concatenate([h2_lat.reshape(B * S, D), h2_ctx.reshape(B * C, D)], axis=0)
            m = moe_ffn(tokens, w_router[l], b_router[l], w_gate[l], b_gate[l], w_up[l], b_up[l],
                        w_down[l], b_down[l])
            x = x + gt2 * m[:B * S].reshape(B, S, D)
            xc = xc + cgt2 * m[B * S:].reshape(B, C, D)
        else:
            m = moe_ffn(h2_lat.reshape(B * S, D), w_router[l], b_router[l], w_gate[l], b_gate[l],
                        w_up[l], b_up[l], w_down[l], b_down[l])
            x = x + gt2 * m.reshape(B, S, D)
    return rmsnorm(x, g_final)
```

```python
import functools

import jax
import jax.numpy as jnp
from jax import lax
from jax.experimental import pallas as pl
from jax.experimental.pallas import tpu as pltpu

F32 = jnp.float32
BF16 = jnp.bfloat16
I32 = jnp.int32

D_MODEL = 1024
GRID_W = 64
EPS = 1e-6
HEAD_DIM = 64
ATTN_W = 512
N_Q_HEADS = 8
N_KV_HEADS = 2
Q_GROUP = 4
KV_W = 128
ROPE_THETA = 10000.0
AXIS_DIM = 32
CONV_W = 256
CONV_K = 31
CONV_HALO = 16
CHUNK_W = 256
CHUNK_HEADS = 4
CHUNK = 128
Q0, K0, V0, CV0, CH0, IN_W = 0, 512, 640, 768, 1280, 1792
N_EXPERTS = 32
TOP_K = 4
SWIGLU_LIMIT = 7.0
SWIGLU_ALPHA = 1.702

LANES = 128
VMEM_LIMIT = 56 * 1024 * 1024

ROW_TILE = 512
ATTN_Q_TILE = 256
CONV_TILE = 512
CONV_ROWS = 64
ROUTER_TILE = 512
DISPATCH_TILE = 512
EXPERT_TILE = 512
COMBINE_TILE = 256


def _cparams(*sem):
    return pltpu.CompilerParams(dimension_semantics=sem, vmem_limit_bytes=VMEM_LIMIT)


def _split_bf16(a):
    hi = a.astype(BF16)
    lo = (a - hi.astype(F32)).astype(BF16)
    return hi, lo


def _dot(a, b):
    return jnp.dot(a, b, preferred_element_type=F32)


def _dot_nt(a, b):
    return lax.dot_general(a, b, (((1,), (1,)), ((), ())), preferred_element_type=F32)


def _dot3(a, b, dot=_dot):
    ah, al = _split_bf16(a)
    bh, bl = _split_bf16(b)
    return dot(ah, bh) + (dot(ah, bl) + dot(al, bh))


def _rmsnorm(x, g):
    return x * lax.rsqrt(jnp.mean(x * x, axis=-1, keepdims=True) + EPS) * g


def _layernorm(x, g, b):
    mu = jnp.mean(x, axis=-1, keepdims=True)
    xc = x - mu
    return xc * lax.rsqrt(jnp.mean(xc * xc, axis=-1, keepdims=True) + EPS) * g + b


def _ada_kernel(c_ref, w_ref, b_ref, o_ref):
    c = c_ref[...]
    s = c * jax.nn.sigmoid(c)
    o_ref[...] = _dot3(s, w_ref[...]) + b_ref[...]


def _ada(cs, w_ada, b_ada):
    depth, d, n6 = w_ada.shape
    r = cs.shape[0]
    tn = 1536
    return pl.pallas_call(
        _ada_kernel,
        out_shape=jax.ShapeDtypeStruct((depth, r, n6), F32),
        grid=(depth, n6 // tn),
        in_specs=[
            pl.BlockSpec((r, d), lambda l, j: (0, 0)),
            pl.BlockSpec((None, d, tn), lambda l, j: (l, 0, j)),
            pl.BlockSpec((None, 1, tn), lambda l, j: (l, 0, j)),
        ],
        out_specs=pl.BlockSpec((None, r, tn), lambda l, j: (l, 0, j)),
        compiler_params=_cparams("arbitrary", "arbitrary"),
        name="ada",
    )(cs, w_ada, b_ada.reshape(depth, 1, n6))


def _head_sumsq(t, bd):
    hi, lo = _split_bf16(t * t)
    return _dot(hi, bd) + _dot(lo, bd)


def _rope(t, cos, sin):
    w = t.shape[1]
    reps = w // LANES
    cosw = jnp.concatenate([cos] * reps, axis=1) if reps > 1 else cos
    sinw = jnp.concatenate([sin] * reps, axis=1) if reps > 1 else sin
    lane = lax.broadcasted_iota(I32, t.shape, 1)
    first = (lane % HEAD_DIM) < (HEAD_DIM // 2)
    rot = jnp.where(first, pltpu.roll(t, w - HEAD_DIM // 2, 1), pltpu.roll(t, HEAD_DIM // 2, 1))
    return t * cosw + rot * sinw


def _modulated(x_ref, sh_ref, sc_ref, g_ref):
    return _rmsnorm(x_ref[...], g_ref[...]) * (1.0 + sc_ref[...]) + sh_ref[...]


def _qk_head(t, g, bd, cos_ref, sin_ref, rope):
    t = t * lax.rsqrt(_head_sumsq(t, bd) * (1.0 / HEAD_DIM) + EPS) * g
    if rope:
        t = _rope(t, cos_ref[...], sin_ref[...])
    return t


def _proj_kernel(x_ref, sh_ref, sc_ref, g1_ref, w_ref, gq_ref, gk_ref, cos_ref, sin_ref, bd_ref,
                 q_ref, kt_ref, v_ref, zc_ref, zs_ref, *, rope):
    h = _modulated(x_ref, sh_ref, sc_ref, g1_ref)
    p = _dot(h.astype(BF16), w_ref[...])
    q = _qk_head(p[:, Q0:K0], gq_ref[...], bd_ref[...], cos_ref, sin_ref, rope)
    q_ref[...] = (q * (HEAD_DIM ** -0.5)).astype(BF16)
    k = _qk_head(p[:, K0:V0], gk_ref[...], bd_ref[0:KV_W, 0:KV_W], cos_ref, sin_ref, rope)
    kt_ref[...] = k.T.astype(BF16)
    v_ref[...] = p[:, V0:CV0].astype(BF16)
    zc_ref[...] = p[:, CV0:CH0]
    zs_ref[...] = p[:, CH0:IN_W]


def _proj_kv_kernel(x_ref, sh_ref, sc_ref, g1_ref, w_ref, gk_ref, bd_ref, kt_ref, v_ref):
    h = _modulated(x_ref, sh_ref, sc_ref, g1_ref)
    p = _dot(h.astype(BF16), w_ref[...])
    k = _qk_head(p[:, 0:KV_W], gk_ref[...], bd_ref[0:KV_W, 0:KV_W], None, None, False)
    kt_ref[...] = k.T.astype(BF16)
    v_ref[...] = p[:, KV_W:2 * KV_W].astype(BF16)


def _mod_spec(layer, row_of, chunk):
    return pl.BlockSpec((None, None, 1, D_MODEL), lambda b, j: (layer, row_of(b), 0, chunk))


def _const_spec(shape):
    return pl.BlockSpec(shape, lambda b, j: tuple(0 for _ in shape))


def _proj(x, mod4, layer, row_of, g1, w_in_b, gq, gk, cos2, sin2, bd, *, rope, tm):
    bsz, s, d = x.shape
    grid = (bsz, s // tm)
    row = lambda w: pl.BlockSpec((None, tm, w), lambda b, j: (b, j, 0))
    return pl.pallas_call(
        functools.partial(_proj_kernel, rope=rope),
        out_shape=(
            jax.ShapeDtypeStruct((bsz, s, ATTN_W), BF16),
            jax.ShapeDtypeStruct((bsz, KV_W, s), BF16),
            jax.ShapeDtypeStruct((bsz, s, KV_W), BF16),
            jax.ShapeDtypeStruct((bsz, s, 2 * CONV_W), F32),
            jax.ShapeDtypeStruct((bsz, s, 2 * CHUNK_W), F32),
        ),
        grid=grid,
        in_specs=[
            row(d),
            _mod_spec(layer, row_of, 0),
            _mod_spec(layer, row_of, 1),
            _const_spec((1, d)),
            _const_spec((d, IN_W)),
            _const_spec((1, ATTN_W)),
            _const_spec((1, KV_W)),
            pl.BlockSpec((tm, LANES), lambda b, j: (j, 0)),
            pl.BlockSpec((tm, LANES), lambda b, j: (j, 0)),
            _const_spec((ATTN_W, ATTN_W)),
        ],
        out_specs=(
            row(ATTN_W),
            pl.BlockSpec((None, KV_W, tm), lambda b, j: (b, 0, j)),
            row(KV_W),
            row(2 * CONV_W),
            row(2 * CHUNK_W),
        ),
        compiler_params=_cparams("arbitrary", "arbitrary"),
        name="proj",
    )(x, mod4, mod4, g1, w_in_b, gq, gk, cos2, sin2, bd)


def _proj_kv(x, mod4, layer, row_of, g1, w_kv_b, gk, bd, *, tm):
    bsz, s, d = x.shape
    return pl.pallas_call(
        _proj_kv_kernel,
        out_shape=(
            jax.ShapeDtypeStruct((bsz, KV_W, s), BF16),
            jax.ShapeDtypeStruct((bsz, s, KV_W), BF16),
        ),
        grid=(bsz, s // tm),
        in_specs=[
            pl.BlockSpec((None, tm, d), lambda b, j: (b, j, 0)),
            _mod_spec(layer, row_of, 0),
            _mod_spec(layer, row_of, 1),
            _const_spec((1, d)),
            _const_spec((d, 2 * KV_W)),
            _const_spec((1, KV_W)),
            _const_spec((ATTN_W, ATTN_W)),
        ],
        out_specs=(
            pl.BlockSpec((None, KV_W, tm), lambda b, j: (b, 0, j)),
            pl.BlockSpec((None, tm, KV_W), lambda b, j: (b, j, 0)),
        ),
        compiler_params=_cparams("arbitrary", "arbitrary"),
        name="proj_kv",
    )(x, mod4, mod4, g1, w_kv_b, gk, bd)


def _attn_kernel(*refs, n_src):
    q_ref = refs[0]
    kt_refs = refs[1:1 + n_src]
    v_refs = refs[1 + n_src:1 + 2 * n_src]
    o_ref = refs[1 + 2 * n_src]
    outs = []
    for h in range(N_Q_HEADS):
        g = h // Q_GROUP
        qh = q_ref[:, h * HEAD_DIM:(h + 1) * HEAD_DIM]
        scores = [_dot(qh, kt[g * HEAD_DIM:(g + 1) * HEAD_DIM, :]) for kt in kt_refs]
        m = functools.reduce(jnp.maximum, [jnp.max(sc, axis=-1, keepdims=True) for sc in scores])
        ps = [jnp.exp(sc - m) for sc in scores]
        denom = functools.reduce(jnp.add, [jnp.sum(p, axis=-1, keepdims=True) for p in ps])
        o = functools.reduce(jnp.add, [_dot(p.astype(BF16), v[...]) for p, v in zip(ps, v_refs)])
        outs.append(o[:, g * HEAD_DIM:(g + 1) * HEAD_DIM] / denom)
    o_ref[...] = jnp.concatenate(outs, axis=1).astype(BF16)


def _attention(q, kts, vs, *, tq):
    bsz, nq, _ = q.shape
    n_src = len(kts)
    in_specs = [pl.BlockSpec((None, tq, ATTN_W), lambda b, j: (b, j, 0))]
    in_specs += [pl.BlockSpec((None, KV_W, kt.shape[2]), lambda b, j: (b, 0, 0)) for kt in kts]
    in_specs += [pl.BlockSpec((None, v.shape[1], KV_W), lambda b, j: (b, 0, 0)) for v in vs]
    return pl.pallas_call(
        functools.partial(_attn_kernel, n_src=n_src),
        out_shape=jax.ShapeDtypeStruct((bsz, nq, ATTN_W), BF16),
        grid=(bsz, nq // tq),
        in_specs=in_specs,
        out_specs=pl.BlockSpec((None, tq, ATTN_W), lambda b, j: (b, j, 0)),
        compiler_params=_cparams("arbitrary", "arbitrary"),
        name="attention",
    )(q, *kts, *vs)


def _glu(z):
    return z[:, :CONV_W] * jax.nn.sigmoid(z[:, CONV_W:])


def _conv_kernel(zp_ref, z_ref, zn_ref, w_ref, bdw_ref, g_ref, b_ref, o_ref, u_ref, *, ts):
    j = pl.program_id(1)
    nj = pl.num_programs(1)
    u_ref[0:CONV_HALO, :] = jnp.where(j > 0, _glu(zp_ref[...]), 0.0)
    u_ref[CONV_HALO + ts:CONV_HALO + ts + CONV_HALO, :] = jnp.where(j < nj - 1, _glu(zn_ref[...]), 0.0)
    u_ref[CONV_HALO:CONV_HALO + ts, :] = _glu(z_ref[...])
    off = CONV_HALO - CONV_K // 2
    for c in range(ts // CONV_ROWS):
        base = c * CONV_ROWS + off
        acc = jnp.zeros((CONV_ROWS, CONV_W), F32)
        for k in range(CONV_K):
            acc = acc + u_ref[base + k:base + k + CONV_ROWS, :] * w_ref[k:k + 1, :]
        y = _layernorm(acc + bdw_ref[...], g_ref[...], b_ref[...])
        o_ref[c * CONV_ROWS:(c + 1) * CONV_ROWS, :] = (y * jax.nn.sigmoid(y)).astype(BF16)


def _conv(zc, w_dw, b_dw, g_ln, b_ln, *, ts):
    bsz, s, _ = zc.shape
    nh = ts // CONV_HALO
    last = s // CONV_HALO - 1
    return pl.pallas_call(
        functools.partial(_conv_kernel, ts=ts),
        out_shape=jax.ShapeDtypeStruct((bsz, s, CONV_W), BF16),
        grid=(bsz, s // ts),
        in_specs=[
            pl.BlockSpec((None, CONV_HALO, 2 * CONV_W), lambda b, j: (b, jnp.maximum(j * nh - 1, 0), 0)),
            pl.BlockSpec((None, ts, 2 * CONV_W), lambda b, j: (b, j, 0)),
            pl.BlockSpec((None, CONV_HALO, 2 * CONV_W), lambda b, j: (b, jnp.minimum((j + 1) * nh, last), 0)),
            _const_spec((CONV_K, CONV_W)),
            _const_spec((1, CONV_W)),
            _const_spec((1, CONV_W)),
            _const_spec((1, CONV_W)),
        ],
        out_specs=pl.BlockSpec((None, ts, CONV_W), lambda b, j: (b, j, 0)),
        scratch_shapes=[pltpu.VMEM((ts + 2 * CONV_HALO, CONV_W), F32)],
        compiler_params=_cparams("arbitrary", "arbitrary"),
        name="conv",
    )(zc, zc, zc, w_dw, b_dw, g_ln, b_ln)


def _sgu_kernel(z_ref, g_ref, b_ref, ws_ref, bs_ref, o_ref, *, tm):
    z = z_ref[...]
    z = 0.5 * z * (1.0 + lax.erf(z * (2.0 ** -0.5)))
    u = z[:, :CHUNK_W]
    v = _layernorm(z[:, CHUNK_W:], g_ref[...], b_ref[...]).astype(BF16)
    head = lax.broadcasted_iota(I32, (CHUNK, CHUNK_W), 1) // (CHUNK_W // CHUNK_HEADS)
    for c in range(tm // CHUNK):
        vc = v[c * CHUNK:(c + 1) * CHUNK, :]
        s = bs_ref[...]
        for h in range(CHUNK_HEADS):
            s = s + jnp.where(head == h, _dot(ws_ref[h], vc), 0.0)
        o_ref[c * CHUNK:(c + 1) * CHUNK, :] = (u[c * CHUNK:(c + 1) * CHUNK, :] * s).astype(BF16)


def _sgu(zs, g_ln, b_ln, ws_b, bs_full, *, tm):
    bsz, s, _ = zs.shape
    return pl.pallas_call(
        functools.partial(_sgu_kernel, tm=tm),
        out_shape=jax.ShapeDtypeStruct((bsz, s, CHUNK_W), BF16),
        grid=(bsz, s // tm),
        in_specs=[
            pl.BlockSpec((None, tm, 2 * CHUNK_W), lambda b, j: (b, j, 0)),
            _const_spec((1, CHUNK_W)),
            _const_spec((1, CHUNK_W)),
            _const_spec((CHUNK_HEADS, CHUNK, CHUNK)),
            _const_spec((CHUNK, CHUNK_W)),
        ],
        out_specs=pl.BlockSpec((None, tm, CHUNK_W), lambda b, j: (b, j, 0)),
        compiler_params=_cparams("arbitrary", "arbitrary"),
        name="sgu",
    )(zs, g_ln, b_ln, ws_b, bs_full)


def _oproj_kernel(a_ref, cv_ref, ch_ref, x_ref, gt_ref, sh_ref, sc_ref, g2_ref, wo_ref, xo_ref, h2_ref):
    y = (_dot(a_ref[...], wo_ref[0:ATTN_W, :])
         + _dot(cv_ref[...], wo_ref[ATTN_W:ATTN_W + CONV_W, :])
         + _dot(ch_ref[...], wo_ref[ATTN_W + CONV_W:, :]))
    xn = x_ref[...] + gt_ref[...] * y
    xo_ref[...] = xn
    h2_ref[...] = _rmsnorm(xn, g2_ref[...]) * (1.0 + sc_ref[...]) + sh_ref[...]


def _oproj(a, cv, ch, x, mod4, layer, row_of, g2, wo_b, *, tm):
    bsz, s, d = x.shape
    row = lambda w: pl.BlockSpec((None, tm, w), lambda b, j: (b, j, 0))
    return pl.pallas_call(
        _oproj_kernel,
        out_shape=(jax.ShapeDtypeStruct((bsz, s, d), F32), jax.ShapeDtypeStruct((bsz, s, d), F32)),
        grid=(bsz, s // tm),
        in_specs=[
            row(ATTN_W), row(CONV_W), row(CHUNK_W), row(d),
            _mod_spec(layer, row_of, 2),
            _mod_spec(layer, row_of, 3),
            _mod_spec(layer, row_of, 4),
            _const_spec((1, d)),
            _const_spec((d, d)),
        ],
        out_specs=(row(d), row(d)),
        compiler_params=_cparams("arbitrary", "arbitrary"),
        name="oproj",
    )(a, cv, ch, x, mod4, mod4, mod4, g2, wo_b)


def _router_kernel(h_ref, wr_ref, br_ref, idx_ref, wt_ref, rank_ref, cnt_ref, carry_ref, *, tb):
    @pl.when(pl.program_id(0) == 0)
    def _():
        carry_ref[...] = jnp.zeros_like(carry_ref)

    logits = _dot3(wr_ref[...], h_ref[...], dot=_dot_nt) + br_ref[...]
    e_iota = lax.broadcasted_iota(I32, (N_EXPERTS, tb), 0)
    vals, idxs = [], []
    work = logits
    for _ in range(TOP_K):
        m = jnp.max(work, axis=0, keepdims=True)
        ik = jnp.min(jnp.where(work == m, e_iota, N_EXPERTS), axis=0, keepdims=True)
        vals.append(m)
        idxs.append(ik)
        work = jnp.where(e_iota == ik, -jnp.inf, work)
    exps = [jnp.exp(v - vals[0]) for v in vals]
    denom = functools.reduce(jnp.add, exps)
    wt_ref[...] = jnp.concatenate([e / denom for e in exps], axis=0)
    idx_ref[...] = jnp.concatenate(idxs, axis=0)

    onehot = functools.reduce(jnp.add, [(e_iota == ik).astype(F32) for ik in idxs])
    upper = (lax.broadcasted_iota(I32, (tb, tb), 0) < lax.broadcasted_iota(I32, (tb, tb), 1)).astype(BF16)
    before = _dot(onehot.astype(BF16), upper) + carry_ref[...]
    ranks = [jnp.sum(jnp.where(e_iota == ik, before, 0.0), axis=0, keepdims=True) for ik in idxs]
    rank_ref[...] = jnp.concatenate(ranks, axis=0).astype(I32)
    carry_ref[...] = carry_ref[...] + jnp.sum(onehot, axis=1, keepdims=True)
    cnt_ref[...] = jnp.broadcast_to(carry_ref[...], cnt_ref.shape)


def _router(h2, wr_t, br, *, tb):
    n, d = h2.shape
    tok = lambda dt: jax.ShapeDtypeStruct((TOP_K, n), dt)
    tok_spec = pl.BlockSpec((TOP_K, tb), lambda i: (0, i))
    return pl.pallas_call(
        functools.partial(_router_kernel, tb=tb),
        out_shape=(tok(I32), tok(F32), tok(I32), jax.ShapeDtypeStruct((N_EXPERTS, LANES), F32)),
        grid=(n // tb,),
        in_specs=[
            pl.BlockSpec((tb, d), lambda i: (i, 0)),
            pl.BlockSpec((N_EXPERTS, d), lambda i: (0, 0)),
            pl.BlockSpec((N_EXPERTS, 1), lambda i: (0, 0)),
        ],
        out_specs=(tok_spec, tok_spec, tok_spec, pl.BlockSpec((N_EXPERTS, LANES), lambda i: (0, 0))),
        scratch_shapes=[pltpu.VMEM((N_EXPERTS, 1), F32)],
        compiler_params=_cparams("arbitrary"),
        name="router",
    )(h2, wr_t, br)


def _pos_kernel(start_ref, idx_ref, rank_ref, pos_ref):
    idx = idx_ref[...]
    pos = rank_ref[...]
    for e in range(N_EXPERTS):
        pos = pos + jnp.where(idx == e, start_ref[e], 0)
    pos_ref[...] = pos


def _positions(starts, idx, rank):
    return pl.pallas_call(
        _pos_kernel,
        out_shape=jax.ShapeDtypeStruct(idx.shape, I32),
        grid_spec=pltpu.PrefetchScalarGridSpec(
            num_scalar_prefetch=1, grid=(1,),
            in_specs=[pl.BlockSpec(idx.shape, lambda i, s: (0, 0)), pl.BlockSpec(idx.shape, lambda i, s: (0, 0))],
            out_specs=pl.BlockSpec(idx.shape, lambda i, s: (0, 0))),
        compiler_params=_cparams("arbitrary"),
        name="positions",
    )(starts, idx, rank)


def _row_copy(src, src_row, dst, dst_row, sem):
    return pltpu.make_async_copy(src.at[pl.ds(src_row, 1)], dst.at[pl.ds(dst_row, 1)], sem)


def _dispatch_kernel(zf_ref, pos_ref, h_ref, xs_ref, zbuf_ref, sem, zsem, *, tb, tm, nblk):
    @pl.when(pl.program_id(0) == 0)
    def _():
        zbuf_ref[...] = jnp.zeros_like(zbuf_ref)

        def zero_copy(i):
            return pltpu.make_async_copy(zbuf_ref, xs_ref.at[pl.ds(pl.multiple_of(i * tm, tm), tm)], zsem)

        def start(i, carry):
            @pl.when(zf_ref[i] != 0)
            def _():
                zero_copy(i).start()
            return carry

        def wait(i, carry):
            @pl.when(zf_ref[i] != 0)
            def _():
                zero_copy(i).wait()
            return carry

        lax.fori_loop(0, nblk, start, 0)
        lax.fori_loop(0, nblk, wait, 0)

    base = pl.program_id(0) * tb

    def issue(t, carry):
        for k in range(TOP_K):
            _row_copy(h_ref, base + t, xs_ref, pos_ref[0, k * tb + t], sem).start()
        return carry

    lax.fori_loop(0, tb, issue, 0)
    pltpu.make_async_copy(xs_ref.at[pl.ds(0, TOP_K * tb)], xs_ref.at[pl.ds(0, TOP_K * tb)], sem).wait()


def _dispatch(zero_flag, pos_blocks, h2, *, tb, tm):
    n, d = h2.shape
    nblk = zero_flag.shape[0]
    return pl.pallas_call(
        functools.partial(_dispatch_kernel, tb=tb, tm=tm, nblk=nblk),
        out_shape=jax.ShapeDtypeStruct((nblk * tm, d), F32),
        grid_spec=pltpu.PrefetchScalarGridSpec(
            num_scalar_prefetch=1, grid=(n // tb,),
            in_specs=[
                pl.BlockSpec((None, 1, TOP_K * tb), lambda i, zf: (i, 0, 0), memory_space=pltpu.SMEM),
                pl.BlockSpec(memory_space=pl.ANY),
            ],
            out_specs=pl.BlockSpec(memory_space=pl.ANY),
            scratch_shapes=[pltpu.VMEM((tm, d), F32), pltpu.SemaphoreType.DMA(()), pltpu.SemaphoreType.DMA(())]),
        compiler_params=_cparams("arbitrary"),
        name="dispatch",
    )(zero_flag, pos_blocks, h2)


def _expert_kernel(be_ref, nb_ref, xs_ref, wg_ref, bg_ref, wu_ref, bu_ref, wd_ref, bd_ref, ys_ref):
    used = pl.program_id(0) < nb_ref[0]

    @pl.when(used)
    def _():
        x = xs_ref[...].astype(BF16)
        g = jnp.minimum(_dot(x, wg_ref[...]) + bg_ref[...], SWIGLU_LIMIT)
        u = jnp.clip(_dot(x, wu_ref[...]) + bu_ref[...], -SWIGLU_LIMIT, SWIGLU_LIMIT)
        a = (u + 1.0) * (g * jax.nn.sigmoid(SWIGLU_ALPHA * g))
        ys_ref[...] = _dot(a.astype(BF16), wd_ref[...]) + bd_ref[...]

    @pl.when(jnp.logical_not(used))
    def _():
        ys_ref[...] = jnp.zeros_like(ys_ref)


def _experts(block_e, n_used, xs, wg, bg, wu, bu, wd, bd, *, tm):
    p, d = xs.shape
    f = wg.shape[2]
    wspec = lambda a, b: pl.BlockSpec((None, a, b), lambda i, be, nb: (be[i], 0, 0))
    return pl.pallas_call(
        _expert_kernel,
        out_shape=jax.ShapeDtypeStruct((p, d), F32),
        grid_spec=pltpu.PrefetchScalarGridSpec(
            num_scalar_prefetch=2, grid=(p // tm,),
            in_specs=[
                pl.BlockSpec((tm, d), lambda i, be, nb: (i, 0)),
                wspec(d, f), wspec(1, f), wspec(d, f), wspec(1, f), wspec(f, d), wspec(1, d),
            ],
            out_specs=pl.BlockSpec((tm, d), lambda i, be, nb: (i, 0))),
        compiler_params=_cparams("arbitrary"),
        name="experts",
    )(block_e, n_used, xs, wg, bg, wu, bu, wd, bd)


def _combine_kernel(pos_ref, ys_ref, wt_ref, x_ref, gt_ref, gf_ref, o_ref, buf_ref, sem, *, tb, final_norm):
    def issue(t, carry):
        for k in range(TOP_K):
            _row_copy(ys_ref, pos_ref[0, k * tb + t], buf_ref.at[k], t, sem).start()
        return carry

    lax.fori_loop(0, tb, issue, 0)
    for k in range(TOP_K):
        pltpu.make_async_copy(ys_ref.at[pl.ds(0, tb)], buf_ref.at[k], sem).wait()
    m = functools.reduce(jnp.add, [buf_ref[k] * wt_ref[:, k:k + 1] for k in range(TOP_K)])
    xn = x_ref[...] + gt_ref[...] * m
    o_ref[...] = _rmsnorm(xn, gf_ref[...]) if final_norm else xn


def _combine(pos_blocks, ys, wt_tok, x, mod4, layer, row_of, g_final, blk0, *, tb, final_norm):
    bsz, s, d = x.shape
    nj = s // tb
    return pl.pallas_call(
        functools.partial(_combine_kernel, tb=tb, final_norm=final_norm),
        out_shape=jax.ShapeDtypeStruct((bsz, s, d), F32),
        grid=(bsz, nj),
        in_specs=[
            pl.BlockSpec((None, 1, TOP_K * tb), lambda b, j: (blk0 + b * nj + j, 0, 0), memory_space=pltpu.SMEM),
            pl.BlockSpec(memory_space=pl.ANY),
            pl.BlockSpec((tb, TOP_K), lambda b, j: (blk0 + b * nj + j, 0)),
            pl.BlockSpec((None, tb, d), lambda b, j: (b, j, 0)),
            _mod_spec(layer, row_of, 5),
            _const_spec((1, d)),
        ],
        out_specs=pl.BlockSpec((None, tb, d), lambda b, j: (b, j, 0)),
        scratch_shapes=[pltpu.VMEM((TOP_K, tb, d), F32), pltpu.SemaphoreType.DMA(())],
        compiler_params=_cparams("arbitrary", "arbitrary"),
        name="combine",
    )(pos_blocks, ys, wt_tok, x, mod4, g_final)


def _block_layout(pos, tb):
    n = pos.shape[1]
    return pos.reshape(TOP_K, n // tb, tb).transpose(1, 0, 2).reshape(n // tb, 1, TOP_K * tb)


def _moe(h2, wr_t, br, wg, bg, wu, bu, wd, bd):
    n = h2.shape[0]
    tm = EXPERT_TILE
    idx, wt, rank, cnt = _router(h2, wr_t, br, tb=ROUTER_TILE)
    counts = cnt[:, 0].astype(I32)
    padded = (counts + tm - 1) // tm * tm
    ends = jnp.cumsum(padded)
    pos = _positions((ends - padded).astype(I32), idx, rank)
    nblk = (n * TOP_K) // tm + N_EXPERTS
    n_used = ends[-1] // tm
    blk = jnp.arange(nblk, dtype=I32)
    block_e = jnp.minimum(jnp.searchsorted(ends, jnp.minimum(blk, n_used - 1) * tm, side="right"),
                          N_EXPERTS - 1).astype(I32)
    partial_or_unused = (blk >= n_used) | jnp.any(((blk + 1) * tm)[:, None] == ends[None, :], axis=1)
    xs = _dispatch(partial_or_unused.astype(I32), _block_layout(pos, DISPATCH_TILE), h2, tb=DISPATCH_TILE, tm=tm)
    ys = _experts(block_e, n_used.reshape(1).astype(I32), xs, wg, bg, wu, bu, wd, bd, tm=tm)
    return ys, pos, wt


def _rope_tables(s):
    pos = jnp.arange(s)
    r, col = (pos // GRID_W).astype(F32), (pos % GRID_W).astype(F32)
    inv = ROPE_THETA ** (-jnp.arange(0, AXIS_DIM, 2, dtype=F32) / AXIS_DIM)
    ang = jnp.concatenate([r[:, None] * inv, col[:, None] * inv], axis=-1)
    cos, sin = jnp.cos(ang), jnp.sin(ang)
    return jnp.tile(jnp.concatenate([cos, cos], axis=-1), (1, 2)), jnp.tile(jnp.concatenate([-sin, sin], axis=-1), (1, 2))


def kernel(x, c, ctx, c_ctx, w_ada, b_ada, g_norm1, w_in, g_q, g_k, w_dw, b_dw, g_conv_ln, b_conv_ln,
           g_sgu_ln, b_sgu_ln, w_s, b_s, w_o, g_norm2, w_router, b_router, w_gate, b_gate, w_up, b_up,
           w_down, b_down, g_final):
    bsz, s, d = x.shape
    cl = ctx.shape[1]
    depth = w_ada.shape[0]
    n_lat, n_ctx = bsz * s, bsz * cl
    ctx_row = 8 * ((bsz + 7) // 8)
    rows = ctx_row + 8

    cs = jnp.zeros((rows, d), F32).at[:bsz].set(c).at[ctx_row].set(c_ctx)
    mod4 = _ada(cs, w_ada, b_ada).reshape(depth, rows, 1, 6 * d)
    lat_row = lambda b: b
    ctx_row_of = lambda b: ctx_row

    cos2, sin2 = _rope_tables(s)
    cosc = jnp.zeros((cl, LANES), F32)
    head_of = jnp.arange(ATTN_W) // HEAD_DIM
    bd = (head_of[:, None] == head_of[None, :]).astype(BF16)

    tm_lat = min(ROW_TILE, s)
    tm_ctx = min(ROW_TILE, cl)
    xc = ctx
    out = None
    for l in range(depth):
        last = l == depth - 1
        g1 = g_norm1[l].reshape(1, d)
        w_in_b = w_in[l].astype(BF16)
        gq = jnp.tile(g_q[l], N_Q_HEADS).reshape(1, ATTN_W)
        gk = jnp.tile(g_k[l], N_KV_HEADS).reshape(1, KV_W)
        conv_args = (w_dw[l], b_dw[l].reshape(1, -1), g_conv_ln[l].reshape(1, -1), b_conv_ln[l].reshape(1, -1))
        sgu_args = (g_sgu_ln[l].reshape(1, -1), b_sgu_ln[l].reshape(1, -1), w_s[l].astype(BF16),
                    jnp.repeat(b_s[l].T, CHUNK_W // CHUNK_HEADS, axis=1))
        wo_b = w_o[l].astype(BF16)
        g2 = g_norm2[l].reshape(1, d)

        q, kt, v, zc, zs = _proj(x, mod4, l, lat_row, g1, w_in_b, gq, gk, cos2, sin2, bd, rope=True, tm=tm_lat)
        if last:
            ktc, vc = _proj_kv(xc, mod4, l, ctx_row_of, g1, w_in_b[:, K0:CV0], gk, bd, tm=tm_ctx)
        else:
            qc, ktc, vc, zcc, zsc = _proj(xc, mod4, l, ctx_row_of, g1, w_in_b, gq, gk, cosc, cosc, bd,
                                          rope=False, tm=tm_ctx)
        a = _attention(q, [kt, ktc], [v, vc], tq=min(ATTN_Q_TILE, s))
        cv = _conv(zc, *conv_args, ts=min(CONV_TILE, s))
        ch = _sgu(zs, *sgu_args, tm=tm_lat)
        x, h2 = _oproj(a, cv, ch, x, mod4, l, lat_row, g2, wo_b, tm=tm_lat)
        tokens = h2.reshape(n_lat, d)
        if not last:
            ac = _attention(qc, [ktc], [vc], tq=min(ATTN_Q_TILE, cl))
            cvc = _conv(zcc, *conv_args, ts=min(CONV_TILE, cl))
            chc = _sgu(zsc, *sgu_args, tm=tm_ctx)
            xc, h2c = _oproj(ac, cvc, chc, xc, mod4, l, ctx_row_of, g2, wo_b, tm=tm_ctx)
            tokens = jnp.concatenate([tokens, h2c.reshape(n_ctx, d)], axis=0)

        ys, pos, wt = _moe(tokens, w_router[l].T, b_router[l].reshape(N_EXPERTS, 1),
                           w_gate[l].astype(BF16), b_gate[l].reshape(N_EXPERTS, 1, -1),
                           w_up[l].astype(BF16), b_up[l].reshape(N_EXPERTS, 1, -1),
                           w_down[l].astype(BF16), b_down[l].reshape(N_EXPERTS, 1, -1))
        tb = min(COMBINE_TILE, cl)
        pos_blocks = _block_layout(pos, tb)
        wt_tok = wt.T
        gf = g_final.reshape(1, d)
        x = _combine(pos_blocks, ys, wt_tok, x, mod4, l, lat_row, gf, 0, tb=tb, final_norm=last)
        if not last:
            xc = _combine(pos_blocks, ys, wt_tok, xc, mod4, l, ctx_row_of, gf, n_lat // tb, tb=tb, final_norm=False)
        out = x
    return out
```

```python
import functools

import jax
import jax.numpy as jnp
from jax import lax
from jax.experimental import pallas as pl
from jax.experimental.pallas import tpu as pltpu

F32 = jnp.float32
BF16 = jnp.bfloat16
I32 = jnp.int32

D_MODEL = 1024
GRID_W = 64
EPS = 1e-6
HEAD_DIM = 64
ATTN_W = 512
N_Q_HEADS = 8
N_KV_HEADS = 2
Q_GROUP = 4
KV_W = 128
ROPE_THETA = 10000.0
AXIS_DIM = 32
CONV_W = 256
CONV_K = 31
CONV_HALO = 16
CHUNK_W = 256
CHUNK_HEADS = 4
CHUNK = 128
Q0, K0, V0, CV0, CH0, IN_W = 0, 512, 640, 768, 1280, 1792
N_EXPERTS = 32
TOP_K = 4
SWIGLU_LIMIT = 7.0
SWIGLU_ALPHA = 1.702

LANES = 128
VMEM_LIMIT = 56 * 1024 * 1024

ROW_TILE = 512
ATTN_Q_TILE = 256
CONV_TILE = 512
CONV_ROWS = 64
ROUTER_TILE = 512
DISPATCH_TILE = 512
EXPERT_TILE = 512
COMBINE_TILE = 256


def _cparams(*sem):
    return pltpu.CompilerParams(dimension_semantics=sem, vmem_limit_bytes=VMEM_LIMIT)


def _split_bf16(a):
    hi = a.astype(BF16)
    lo = (a - hi.astype(F32)).astype(BF16)
    return hi, lo


def _dot(a, b):
    return jnp.dot(a, b, preferred_element_type=F32)


def _dot_nt(a, b):
    return lax.dot_general(a, b, (((1,), (1,)), ((), ())), preferred_element_type=F32)


def _dot3(a, b, dot=_dot):
    ah, al = _split_bf16(a)
    bh, bl = _split_bf16(b)
    return dot(ah, bh) + (dot(ah, bl) + dot(al, bh))


def _rmsnorm(x, g):
    return x * lax.rsqrt(jnp.mean(x * x, axis=-1, keepdims=True) + EPS) * g


def _layernorm(x, g, b):
    mu = jnp.mean(x, axis=-1, keepdims=True)
    xc = x - mu
    return xc * lax.rsqrt(jnp.mean(xc * xc, axis=-1, keepdims=True) + EPS) * g + b


def _ada_kernel(c_ref, w_ref, b_ref, o_ref):
    c = c_ref[...]
    s = c * jax.nn.sigmoid(c)
    o_ref[...] = _dot3(s, w_ref[...]) + b_ref[...]


def _ada(cs, w_ada, b_ada):
    depth, d, n6 = w_ada.shape
    r = cs.shape[0]
    tn = 1536
    return pl.pallas_call(
        _ada_kernel,
        out_shape=jax.ShapeDtypeStruct((depth, r, n6), F32),
        grid=(depth, n6 // tn),
        in_specs=[
            pl.BlockSpec((r, d), lambda l, j: (0, 0)),
            pl.BlockSpec((None, d, tn), lambda l, j: (l, 0, j)),
            pl.BlockSpec((None, 1, tn), lambda l, j: (l, 0, j)),
        ],
        out_specs=pl.BlockSpec((None, r, tn), lambda l, j: (l, 0, j)),
        compiler_params=_cparams("arbitrary", "arbitrary"),
        name="ada",
    )(cs, w_ada, b_ada.reshape(depth, 1, n6))


def _head_sumsq(t, bd):
    hi, lo = _split_bf16(t * t)
    return _dot(hi, bd) + _dot(lo, bd)


def _rope(t, cos, sin):
    w = t.shape[1]
    reps = w // LANES
    cosw = jnp.concatenate([cos] * reps, axis=1) if reps > 1 else cos
    sinw = jnp.concatenate([sin] * reps, axis=1) if reps > 1 else sin
    lane = lax.broadcasted_iota(I32, t.shape, 1)
    first = (lane % HEAD_DIM) < (HEAD_DIM // 2)
    rot = jnp.where(first, pltpu.roll(t, w - HEAD_DIM // 2, 1), pltpu.roll(t, HEAD_DIM // 2, 1))
    return t * cosw + rot * sinw


def _modulated(x_ref, sh_ref, sc_ref, g_ref):
    return _rmsnorm(x_ref[...], g_ref[...]) * (1.0 + sc_ref[...]) + sh_ref[...]


def _qk_head(t, g, bd, cos_ref, sin_ref, rope):
    t = t * lax.rsqrt(_head_sumsq(t, bd) * (1.0 / HEAD_DIM) + EPS) * g
    if rope:
        t = _rope(t, cos_ref[...], sin_ref[...])
    return t


def _proj_kernel(x_ref, sh_ref, sc_ref, g1_ref, w_ref, gq_ref, gk_ref, cos_ref, sin_ref, bd_ref,
                 q_ref, kt_ref, v_ref, zc_ref, zs_ref, *, rope):
    h = _modulated(x_ref, sh_ref, sc_ref, g1_ref)
    p = _dot(h.astype(BF16), w_ref[...])
    q = _qk_head(p[:, Q0:K0], gq_ref[...], bd_ref[...], cos_ref, sin_ref, rope)
    q_ref[...] = (q * (HEAD_DIM ** -0.5)).astype(BF16)
    k = _qk_head(p[:, K0:V0], gk_ref[...], bd_ref[0:KV_W, 0:KV_W], cos_ref, sin_ref, rope)
    kt_ref[...] = k.T.astype(BF16)
    v_ref[...] = p[:, V0:CV0].astype(BF16)
    zc_ref[...] = p[:, CV0:CH0]
    zs_ref[...] = p[:, CH0:IN_W]


def _proj_kv_kernel(x_ref, sh_ref, sc_ref, g1_ref, w_ref, gk_ref, bd_ref, kt_ref, v_ref):
    h = _modulated(x_ref, sh_ref, sc_ref, g1_ref)
    p = _dot(h.astype(BF16), w_ref[...])
    k = _qk_head(p[:, 0:KV_W], gk_ref[...], bd_ref[0:KV_W, 0:KV_W], None, None, False)
    kt_ref[...] = k.T.astype(BF16)
    v_ref[...] = p[:, KV_W:2 * KV_W].astype(BF16)


def _mod_spec(layer, row_of, chunk):
    return pl.BlockSpec((None, None, 1, D_MODEL), lambda b, j: (layer, row_of(b), 0, chunk))


def _const_spec(shape):
    return pl.BlockSpec(shape, lambda b, j: tuple(0 for _ in shape))


def _proj(x, mod4, layer, row_of, g1, w_in_b, gq, gk, cos2, sin2, bd, *, rope, tm):
    bsz, s, d = x.shape
    grid = (bsz, s // tm)
    row = lambda w: pl.BlockSpec((None, tm, w), lambda b, j: (b, j, 0))
    return pl.pallas_call(
        functools.partial(_proj_kernel, rope=rope),
        out_shape=(
            jax.ShapeDtypeStruct((bsz, s, ATTN_W), BF16),
            jax.ShapeDtypeStruct((bsz, KV_W, s), BF16),
            jax.ShapeDtypeStruct((bsz, s, KV_W), BF16),
            jax.ShapeDtypeStruct((bsz, s, 2 * CONV_W), F32),
            jax.ShapeDtypeStruct((bsz, s, 2 * CHUNK_W), F32),
        ),
        grid=grid,
        in_specs=[
            row(d),
            _mod_spec(layer, row_of, 0),
            _mod_spec(layer, row_of, 1),
            _const_spec((1, d)),
            _const_spec((d, IN_W)),
            _const_spec((1, ATTN_W)),
            _const_spec((1, KV_W)),
            pl.BlockSpec((tm, LANES), lambda b, j: (j, 0)),
            pl.BlockSpec((tm, LANES), lambda b, j: (j, 0)),
            _const_spec((ATTN_W, ATTN_W)),
        ],
        out_specs=(
            row(ATTN_W),
            pl.BlockSpec((None, KV_W, tm), lambda b, j: (b, 0, j)),
            row(KV_W),
            row(2 * CONV_W),
            row(2 * CHUNK_W),
        ),
        compiler_params=_cparams("arbitrary", "arbitrary"),
        name="proj",
    )(x, mod4, mod4, g1, w_in_b, gq, gk, cos2, sin2, bd)


def _proj_kv(x, mod4, layer, row_of, g1, w_kv_b, gk, bd, *, tm):
    bsz, s, d = x.shape
    return pl.pallas_call(
        _proj_kv_kernel,
        out_shape=(
            jax.ShapeDtypeStruct((bsz, KV_W, s), BF16),
            jax.ShapeDtypeStruct((bsz, s, KV_W), BF16),
        ),
        grid=(bsz, s // tm),
        in_specs=[
            pl.BlockSpec((None, tm, d), lambda b, j: (b, j, 0)),
            _mod_spec(layer, row_of, 0),
            _mod_spec(layer, row_of, 1),
            _const_spec((1, d)),
            _const_spec((d, 2 * KV_W)),
            _const_spec((1, KV_W)),
            _const_spec((ATTN_W, ATTN_W)),
        ],
        out_specs=(
            pl.BlockSpec((None, KV_W, tm), lambda b, j: (b, 0, j)),
            pl.BlockSpec((None, tm, KV_W), lambda b, j: (b, j, 0)),
        ),
        compiler_params=_cparams("arbitrary", "arbitrary"),
        name="proj_kv",
    )(x, mod4, mod4, g1, w_kv_b, gk, bd)


def _attn_kernel(*refs, n_src):
    q_ref = refs[0]
    kt_refs = refs[1:1 + n_src]
    v_refs = refs[1 + n_src:1 + 2 * n_src]
    o_ref = refs[1 + 2 * n_src]
    outs = []
    for h in range(N_Q_HEADS):
        g = h // Q_GROUP
        qh = q_ref[:, h * HEAD_DIM:(h + 1) * HEAD_DIM]
        scores = [_dot(qh, kt[g * HEAD_DIM:(g + 1) * HEAD_DIM, :]) for kt in kt_refs]
        m = functools.reduce(jnp.maximum, [jnp.max(sc, axis=-1, keepdims=True) for sc in scores])
        ps = [jnp.exp(sc - m) for sc in scores]
        denom = functools.reduce(jnp.add, [jnp.sum(p, axis=-1, keepdims=True) for p in ps])
        o = functools.reduce(jnp.add, [_dot(p.astype(BF16), v[...]) for p, v in zip(ps, v_refs)])
        outs.append(o[:, g * HEAD_DIM:(g + 1) * HEAD_DIM] / denom)
    o_ref[...] = jnp.concatenate(outs, axis=1).astype(BF16)


def _attention(q, kts, vs, *, tq):
    bsz, nq, _ = q.shape
    n_src = len(kts)
    in_specs = [pl.BlockSpec((None, tq, ATTN_W), lambda b, j: (b, j, 0))]
    in_specs += [pl.BlockSpec((None, KV_W, kt.shape[2]), lambda b, j: (b, 0, 0)) for kt in kts]
    in_specs += [pl.BlockSpec((None, v.shape[1], KV_W), lambda b, j: (b, 0, 0)) for v in vs]
    return pl.pallas_call(
        functools.partial(_attn_kernel, n_src=n_src),
        out_shape=jax.ShapeDtypeStruct((bsz, nq, ATTN_W), BF16),
        grid=(bsz, nq // tq),
        in_specs=in_specs,
        out_specs=pl.BlockSpec((None, tq, ATTN_W), lambda b, j: (b, j, 0)),
        compiler_params=_cparams("arbitrary", "arbitrary"),
        name="attention",
    )(q, *kts, *vs)


def _glu(z):
    return z[:, :CONV_W] * jax.nn.sigmoid(z[:, CONV_W:])


def _conv_kernel(zp_ref, z_ref, zn_ref, w_ref, bdw_ref, g_ref, b_ref, o_ref, u_ref, *, ts):
    j = pl.program_id(1)
    nj = pl.num_programs(1)
    u_ref[0:CONV_HALO, :] = jnp.where(j > 0, _glu(zp_ref[...]), 0.0)
    u_ref[CONV_HALO + ts:CONV_HALO + ts + CONV_HALO, :] = jnp.where(j < nj - 1, _glu(zn_ref[...]), 0.0)
    u_ref[CONV_HALO:CONV_HALO + ts, :] = _glu(z_ref[...])
    off = CONV_HALO - CONV_K // 2
    for c in range(ts // CONV_ROWS):
        base = c * CONV_ROWS + off
        acc = jnp.zeros((CONV_ROWS, CONV_W), F32)
        for k in range(CONV_K):
            acc = acc + u_ref[base + k:base + k + CONV_ROWS, :] * w_ref[k:k + 1, :]
        y = _layernorm(acc + bdw_ref[...], g_ref[...], b_ref[...])
        o_ref[c * CONV_ROWS:(c + 1) * CONV_ROWS, :] = (y * jax.nn.sigmoid(y)).astype(BF16)


def _conv(zc, w_dw, b_dw, g_ln, b_ln, *, ts):
    bsz, s, _ = zc.shape
    nh = ts // CONV_HALO
    last = s // CONV_HALO - 1
    return pl.pallas_call(
        functools.partial(_conv_kernel, ts=ts),
        out_shape=jax.ShapeDtypeStruct((bsz, s, CONV_W), BF16),
        grid=(bsz, s // ts),
        in_specs=[
            pl.BlockSpec((None, CONV_HALO, 2 * CONV_W), lambda b, j: (b, jnp.maximum(j * nh - 1, 0), 0)),
            pl.BlockSpec((None, ts, 2 * CONV_W), lambda b, j: (b, j, 0)),
            pl.BlockSpec((None, CONV_HALO, 2 * CONV_W), lambda b, j: (b, jnp.minimum((j + 1) * nh, last), 0)),
            _const_spec((CONV_K, CONV_W)),
            _const_spec((1, CONV_W)),
            _const_spec((1, CONV_W)),
            _const_spec((1, CONV_W)),
        ],
        out_specs=pl.BlockSpec((None, ts, CONV_W), lambda b, j: (b, j, 0)),
        scratch_shapes=[pltpu.VMEM((ts + 2 * CONV_HALO, CONV_W), F32)],
        compiler_params=_cparams("arbitrary", "arbitrary"),
        name="conv",
    )(zc, zc, zc, w_dw, b_dw, g_ln, b_ln)


def _sgu_kernel(z_ref, g_ref, b_ref, ws_ref, bs_ref, o_ref, *, tm):
    z = z_ref[...]
    z = 0.5 * z * (1.0 + lax.erf(z * (2.0 ** -0.5)))
    u = z[:, :CHUNK_W]
    v = _layernorm(z[:, CHUNK_W:], g_ref[...], b_ref[...]).astype(BF16)
    head = lax.broadcasted_iota(I32, (CHUNK, CHUNK_W), 1) // (CHUNK_W // CHUNK_HEADS)
    for c in range(tm // CHUNK):
        vc = v[c * CHUNK:(c + 1) * CHUNK, :]
        s = bs_ref[...]
        for h in range(CHUNK_HEADS):
            s = s + jnp.where(head == h, _dot(ws_ref[h], vc), 0.0)
        o_ref[c * CHUNK:(c + 1) * CHUNK, :] = (u[c * CHUNK:(c + 1) * CHUNK, :] * s).astype(BF16)


def _sgu(zs, g_ln, b_ln, ws_b, bs_full, *, tm):
    bsz, s, _ = zs.shape
    return pl.pallas_call(
        functools.partial(_sgu_kernel, tm=tm),
        out_shape=jax.ShapeDtypeStruct((bsz, s, CHUNK_W), BF16),
        grid=(bsz, s // tm),
        in_specs=[
            pl.BlockSpec((None, tm, 2 * CHUNK_W), lambda b, j: (b, j, 0)),
            _const_spec((1, CHUNK_W)),
            _const_spec((1, CHUNK_W)),
            _const_spec((CHUNK_HEADS, CHUNK, CHUNK)),
            _const_spec((CHUNK, CHUNK_W)),
        ],
        out_specs=pl.BlockSpec((None, tm, CHUNK_W), lambda b, j: (b, j, 0)),
        compiler_params=_cparams("arbitrary", "arbitrary"),
        name="sgu",
    )(zs, g_ln, b_ln, ws_b, bs_full)


def _oproj_kernel(a_ref, cv_ref, ch_ref, x_ref, gt_ref, sh_ref, sc_ref, g2_ref, wo_ref, xo_ref, h2_ref):
    y = (_dot(a_ref[...], wo_ref[0:ATTN_W, :])
         + _dot(cv_ref[...], wo_ref[ATTN_W:ATTN_W + CONV_W, :])
         + _dot(ch_ref[...], wo_ref[ATTN_W + CONV_W:, :]))
    xn = x_ref[...] + gt_ref[...] * y
    xo_ref[...] = xn
    h2_ref[...] = _rmsnorm(xn, g2_ref[...]) * (1.0 + sc_ref[...]) + sh_ref[...]


def _oproj(a, cv, ch, x, mod4, layer, row_of, g2, wo_b, *, tm):
    bsz, s, d = x.shape
    row = lambda w: pl.BlockSpec((None, tm, w), lambda b, j: (b, j, 0))
    return pl.pallas_call(
        _oproj_kernel,
        out_shape=(jax.ShapeDtypeStruct((bsz, s, d), F32), jax.ShapeDtypeStruct((bsz, s, d), F32)),
        grid=(bsz, s // tm),
        in_specs=[
            row(ATTN_W), row(CONV_W), row(CHUNK_W), row(d),
            _mod_spec(layer, row_of, 2),
            _mod_spec(layer, row_of, 3),
            _mod_spec(layer, row_of, 4),
            _const_spec((1, d)),
            _const_spec((d, d)),
        ],
        out_specs=(row(d), row(d)),
        compiler_params=_cparams("arbitrary", "arbitrary"),
        name="oproj",
    )(a, cv, ch, x, mod4, mod4, mod4, g2, wo_b)


def _router_kernel(h_ref, wr_ref, br_ref, idx_ref, wt_ref, rank_ref, cnt_ref, carry_ref, *, tb):
    @pl.when(pl.program_id(0) == 0)
    def _():
        carry_ref[...] = jnp.zeros_like(carry_ref)

    logits = _dot3(wr_ref[...], h_ref[...], dot=_dot_nt) + br_ref[...]
    e_iota = lax.broadcasted_iota(I32, (N_EXPERTS, tb), 0)
    vals, idxs = [], []
    work = logits
    for _ in range(TOP_K):
        m = jnp.max(work, axis=0, keepdims=True)
        ik = jnp.min(jnp.where(work == m, e_iota, N_EXPERTS), axis=0, keepdims=True)
        vals.append(m)
        idxs.append(ik)
        work = jnp.where(e_iota == ik, -jnp.inf, work)
    exps = [jnp.exp(v - vals[0]) for v in vals]
    denom = functools.reduce(jnp.add, exps)
    wt_ref[...] = jnp.concatenate([e / denom for e in exps], axis=0)
    idx_ref[...] = jnp.concatenate(idxs, axis=0)

    onehot = functools.reduce(jnp.add, [(e_iota == ik).astype(F32) for ik in idxs])
    upper = (lax.broadcasted_iota(I32, (tb, tb), 0) < lax.broadcasted_iota(I32, (tb, tb), 1)).astype(BF16)
    before = _dot(onehot.astype(BF16), upper) + carry_ref[...]
    ranks = [jnp.sum(jnp.where(e_iota == ik, before, 0.0), axis=0, keepdims=True) for ik in idxs]
    rank_ref[...] = jnp.concatenate(ranks, axis=0).astype(I32)
    carry_ref[...] = carry_ref[...] + jnp.sum(onehot, axis=1, keepdims=True)
    cnt_ref[...] = jnp.broadcast_to(carry_ref[...], cnt_ref.shape)


def _router(h2, wr_t, br, *, tb):
    n, d = h2.shape
    tok = lambda dt: jax.ShapeDtypeStruct((TOP_K, n), dt)
    tok_spec = pl.BlockSpec((TOP_K, tb), lambda i: (0, i))
    return pl.pallas_call(
        functools.partial(_router_kernel, tb=tb),
        out_shape=(tok(I32), tok(F32), tok(I32), jax.ShapeDtypeStruct((N_EXPERTS, LANES), F32)),
        grid=(n // tb,),
        in_specs=[
            pl.BlockSpec((tb, d), lambda i: (i, 0)),
            pl.BlockSpec((N_EXPERTS, d), lambda i: (0, 0)),
            pl.BlockSpec((N_EXPERTS, 1), lambda i: (0, 0)),
        ],
        out_specs=(tok_spec, tok_spec, tok_spec, pl.BlockSpec((N_EXPERTS, LANES), lambda i: (0, 0))),
        scratch_shapes=[pltpu.VMEM((N_EXPERTS, 1), F32)],
        compiler_params=_cparams("arbitrary"),
        name="router",
    )(h2, wr_t, br)


def _pos_kernel(start_ref, idx_ref, rank_ref, pos_ref):
    idx = idx_ref[...]
    pos = rank_ref[...]
    for e in range(N_EXPERTS):
        pos = pos + jnp.where(idx == e, start_ref[e], 0)
    pos_ref[...] = pos


def _positions(starts, idx, rank):
    return pl.pallas_call(
        _pos_kernel,
        out_shape=jax.ShapeDtypeStruct(idx.shape, I32),
        grid_spec=pltpu.PrefetchScalarGridSpec(
            num_scalar_prefetch=1, grid=(1,),
            in_specs=[pl.BlockSpec(idx.shape, lambda i, s: (0, 0)), pl.BlockSpec(idx.shape, lambda i, s: (0, 0))],
            out_specs=pl.BlockSpec(idx.shape, lambda i, s: (0, 0))),
        compiler_params=_cparams("arbitrary"),
        name="positions",
    )(starts, idx, rank)


def _row_copy(src, src_row, dst, dst_row, sem):
    return pltpu.make_async_copy(src.at[pl.ds(src_row, 1)], dst.at[pl.ds(dst_row, 1)], sem)


def _dispatch_kernel(zf_ref, pos_ref, h_ref, xs_ref, zbuf_ref, sem, zsem, *, tb, tm, nblk):
    @pl.when(pl.program_id(0) == 0)
    def _():
        zbuf_ref[...] = jnp.zeros_like(zbuf_ref)

        def zero_copy(i):
            return pltpu.make_async_copy(zbuf_ref, xs_ref.at[pl.ds(pl.multiple_of(i * tm, tm), tm)], zsem)

        def start(i, carry):
            @pl.when(zf_ref[i] != 0)
            def _():
                zero_copy(i).start()
            return carry

        def wait(i, carry):
            @pl.when(zf_ref[i] != 0)
            def _():
                zero_copy(i).wait()
            return carry

        lax.fori_loop(0, nblk, start, 0)
        lax.fori_loop(0, nblk, wait, 0)

    def issue(t, carry):
        for k in range(TOP_K):
            _row_copy(h_ref, t, xs_ref, pos_ref[0, k * tb + t], sem).start()
        return carry

    lax.fori_loop(0, tb, issue, 0)
    pltpu.make_async_copy(xs_ref.at[pl.ds(0, TOP_K * tb)], xs_ref.at[pl.ds(0, TOP_K * tb)], sem).wait()


def _dispatch(zero_flag, pos_blocks, h2, *, tb, tm):
    n, d = h2.shape
    nblk = zero_flag.shape[0]
    return pl.pallas_call(
        functools.partial(_dispatch_kernel, tb=tb, tm=tm, nblk=nblk),
        out_shape=jax.ShapeDtypeStruct((nblk * tm, d), F32),
        grid_spec=pltpu.PrefetchScalarGridSpec(
            num_scalar_prefetch=1, grid=(n // tb,),
            in_specs=[
                pl.BlockSpec((None, 1, TOP_K * tb), lambda i, zf: (i, 0, 0), memory_space=pltpu.SMEM),
                pl.BlockSpec((tb, d), lambda i, zf: (i, 0)),
            ],
            out_specs=pl.BlockSpec(memory_space=pl.ANY),
            scratch_shapes=[pltpu.VMEM((tm, d), F32), pltpu.SemaphoreType.DMA(()), pltpu.SemaphoreType.DMA(())]),
        compiler_params=_cparams("arbitrary"),
        name="dispatch",
    )(zero_flag, pos_blocks, h2)


def _expert_kernel(be_ref, nb_ref, xs_ref, wg_ref, bg_ref, wu_ref, bu_ref, wd_ref, bd_ref, ys_ref):
    used = pl.program_id(0) < nb_ref[0]

    @pl.when(used)
    def _():
        x = xs_ref[...].astype(BF16)
        g = jnp.minimum(_dot(x, wg_ref[...]) + bg_ref[...], SWIGLU_LIMIT)
        u = jnp.clip(_dot(x, wu_ref[...]) + bu_ref[...], -SWIGLU_LIMIT, SWIGLU_LIMIT)
        a = (u + 1.0) * (g * jax.nn.sigmoid(SWIGLU_ALPHA * g))
        ys_ref[...] = _dot(a.astype(BF16), wd_ref[...]) + bd_ref[...]

    @pl.when(jnp.logical_not(used))
    def _():
        ys_ref[...] = jnp.zeros_like(ys_ref)


def _experts(block_e, n_used, xs, wg, bg, wu, bu, wd, bd, *, tm):
    p, d = xs.shape
    f = wg.shape[2]
    wspec = lambda a, b: pl.BlockSpec((None, a, b), lambda i, be, nb: (be[i], 0, 0))
    return pl.pallas_call(
        _expert_kernel,
        out_shape=jax.ShapeDtypeStruct((p, d), F32),
        grid_spec=pltpu.PrefetchScalarGridSpec(
            num_scalar_prefetch=2, grid=(p // tm,),
            in_specs=[
                pl.BlockSpec((tm, d), lambda i, be, nb: (i, 0)),
                wspec(d, f), wspec(1, f), wspec(d, f), wspec(1, f), wspec(f, d), wspec(1, d),
            ],
            out_specs=pl.BlockSpec((tm, d), lambda i, be, nb: (i, 0))),
        compiler_params=_cparams("arbitrary"),
        name="experts",
    )(block_e, n_used, xs, wg, bg, wu, bu, wd, bd)


def _combine_kernel(pos_ref, ys_ref, wt_ref, x_ref, gt_ref, gf_ref, o_ref, buf_ref, sem, *, tb, final_norm):
    def issue(t, carry):
        for k in range(TOP_K):
            _row_copy(ys_ref, pos_ref[0, k * tb + t], buf_ref.at[k], t, sem).start()
        return carry

    lax.fori_loop(0, tb, issue, 0)
    for k in range(TOP_K):
        pltpu.make_async_copy(ys_ref.at[pl.ds(0, tb)], buf_ref.at[k], sem).wait()
    m = functools.reduce(jnp.add, [buf_ref[k] * wt_ref[:, k:k + 1] for k in range(TOP_K)])
    xn = x_ref[...] + gt_ref[...] * m
    o_ref[...] = _rmsnorm(xn, gf_ref[...]) if final_norm else xn


def _combine(pos_blocks, ys, wt_tok, x, mod4, layer, row_of, g_final, blk0, *, tb, final_norm):
    bsz, s, d = x.shape
    nj = s // tb
    return pl.pallas_call(
        functools.partial(_combine_kernel, tb=tb, final_norm=final_norm),
        out_shape=jax.ShapeDtypeStruct((bsz, s, d), F32),
        grid=(bsz, nj),
        in_specs=[
            pl.BlockSpec((None, 1, TOP_K * tb), lambda b, j: (blk0 + b * nj + j, 0, 0), memory_space=pltpu.SMEM),
            pl.BlockSpec(memory_space=pl.ANY),
            pl.BlockSpec((tb, TOP_K), lambda b, j: (blk0 + b * nj + j, 0)),
            pl.BlockSpec((None, tb, d), lambda b, j: (b, j, 0)),
            _mod_spec(layer, row_of, 5),
            _const_spec((1, d)),
        ],
        out_specs=pl.BlockSpec((None, tb, d), lambda b, j: (b, j, 0)),
        scratch_shapes=[pltpu.VMEM((TOP_K, tb, d), F32), pltpu.SemaphoreType.DMA(())],
        compiler_params=_cparams("arbitrary", "arbitrary"),
        name="combine",
    )(pos_blocks, ys, wt_tok, x, mod4, g_final)


def _block_layout(pos, tb):
    n = pos.shape[1]
    return pos.reshape(TOP_K, n // tb, tb).transpose(1, 0, 2).reshape(n // tb, 1, TOP_K * tb)


def _moe(h2, wr_t, br, wg, bg, wu, bu, wd, bd):
    n = h2.shape[0]
    tm = EXPERT_TILE
    idx, wt, rank, cnt = _router(h2, wr_t, br, tb=ROUTER_TILE)
    counts = cnt[:, 0].astype(I32)
    padded = (counts + tm - 1) // tm * tm
    ends = jnp.cumsum(padded)
    pos = _positions((ends - padded).astype(I32), idx, rank)
    nblk = (n * TOP_K) // tm + N_EXPERTS
    n_used = ends[-1] // tm
    blk = jnp.arange(nblk, dtype=I32)
    first_row = jnp.minimum(blk, n_used - 1) * tm
    block_e = jnp.minimum(jnp.sum(ends[None, :] <= first_row[:, None], axis=1), N_EXPERTS - 1).astype(I32)
    partial_or_unused = (blk >= n_used) | jnp.any(((blk + 1) * tm)[:, None] == ends[None, :], axis=1)
    xs = _dispatch(partial_or_unused.astype(I32), _block_layout(pos, DISPATCH_TILE), h2, tb=DISPATCH_TILE, tm=tm)
    ys = _experts(block_e, n_used.reshape(1).astype(I32), xs, wg, bg, wu, bu, wd, bd, tm=tm)
    return ys, pos, wt


def _rope_tables(s):
    pos = jnp.arange(s)
    r, col = (pos // GRID_W).astype(F32), (pos % GRID_W).astype(F32)
    inv = ROPE_THETA ** (-jnp.arange(0, AXIS_DIM, 2, dtype=F32) / AXIS_DIM)
    ang = jnp.concatenate([r[:, None] * inv, col[:, None] * inv], axis=-1)
    cos, sin = jnp.cos(ang), jnp.sin(ang)
    return jnp.tile(jnp.concatenate([cos, cos], axis=-1), (1, 2)), jnp.tile(jnp.concatenate([-sin, sin], axis=-1), (1, 2))


def kernel(x, c, ctx, c_ctx, w_ada, b_ada, g_norm1, w_in, g_q, g_k, w_dw, b_dw, g_conv_ln, b_conv_ln,
           g_sgu_ln, b_sgu_ln, w_s, b_s, w_o, g_norm2, w_router, b_router, w_gate, b_gate, w_up, b_up,
           w_down, b_down, g_final):
    bsz, s, d = x.shape
    cl = ctx.shape[1]
    depth = w_ada.shape[0]
    n_lat, n_ctx = bsz * s, bsz * cl
    ctx_row = 8 * ((bsz + 7) // 8)
    rows = ctx_row + 8

    cs = jnp.zeros((rows, d), F32).at[:bsz].set(c).at[ctx_row].set(c_ctx)
    mod4 = _ada(cs, w_ada, b_ada).reshape(depth, rows, 1, 6 * d)
    lat_row = lambda b: b
    ctx_row_of = lambda b: ctx_row

    cos2, sin2 = _rope_tables(s)
    cosc = jnp.zeros((cl, LANES), F32)
    head_of = jnp.arange(ATTN_W) // HEAD_DIM
    bd = (head_of[:, None] == head_of[None, :]).astype(BF16)

    tm_lat = min(ROW_TILE, s)
    tm_ctx = min(ROW_TILE, cl)
    xc = ctx
    out = None
    for l in range(depth):
        last = l == depth - 1
        g1 = g_norm1[l].reshape(1, d)
        w_in_b = w_in[l].astype(BF16)
        gq = jnp.tile(g_q[l], N_Q_HEADS).reshape(1, ATTN_W)
        gk = jnp.tile(g_k[l], N_KV_HEADS).reshape(1, KV_W)
        conv_args = (w_dw[l], b_dw[l].reshape(1, -1), g_conv_ln[l].reshape(1, -1), b_conv_ln[l].reshape(1, -1))
        sgu_args = (g_sgu_ln[l].reshape(1, -1), b_sgu_ln[l].reshape(1, -1), w_s[l].astype(BF16),
                    jnp.repeat(b_s[l].T, CHUNK_W // CHUNK_HEADS, axis=1))
        wo_b = w_o[l].astype(BF16)
        g2 = g_norm2[l].reshape(1, d)

        q, kt, v, zc, zs = _proj(x, mod4, l, lat_row, g1, w_in_b, gq, gk, cos2, sin2, bd, rope=True, tm=tm_lat)
        if last:
            ktc, vc = _proj_kv(xc, mod4, l, ctx_row_of, g1, w_in_b[:, K0:CV0], gk, bd, tm=tm_ctx)
        else:
            qc, ktc, vc, zcc, zsc = _proj(xc, mod4, l, ctx_row_of, g1, w_in_b, gq, gk, cosc, cosc, bd,
                                          rope=False, tm=tm_ctx)
        a = _attention(q, [kt, ktc], [v, vc], tq=min(ATTN_Q_TILE, s))
        cv = _conv(zc, *conv_args, ts=min(CONV_TILE, s))
        ch = _sgu(zs, *sgu_args, tm=tm_lat)
        x, h2 = _oproj(a, cv, ch, x, mod4, l, lat_row, g2, wo_b, tm=tm_lat)
        tokens = h2.reshape(n_lat, d)
        if not last:
            ac = _attention(qc, [ktc], [vc], tq=min(ATTN_Q_TILE, cl))
            cvc = _conv(zcc, *conv_args, ts=min(CONV_TILE, cl))
            chc = _sgu(zsc, *sgu_args, tm=tm_ctx)
            xc, h2c = _oproj(ac, cvc, chc, xc, mod4, l, ctx_row_of, g2, wo_b, tm=tm_ctx)
            tokens = jnp.concatenate([tokens, h2c.reshape(n_ctx, d)], axis=0)

        ys, pos, wt = _moe(tokens, w_router[l].T, b_router[l].reshape(N_EXPERTS, 1),
                           w_gate[l].astype(BF16), b_gate[l].reshape(N_EXPERTS, 1, -1),
                           w_up[l].astype(BF16), b_up[l].reshape(N_EXPERTS, 1, -1),
                           w_down[l].astype(BF16), b_down[l].reshape(N_EXPERTS, 1, -1))
        tb = min(COMBINE_TILE, cl)
        pos_blocks = _block_layout(pos, tb)
        wt_tok = wt.T
        gf = g_final.reshape(1, d)
        x = _combine(pos_blocks, ys, wt_tok, x, mod4, l, lat_row, gf, 0, tb=tb, final_norm=last)
        if not last:
            xc = _combine(pos_blocks, ys, wt_tok, xc, mod4, l, ctx_row_of, gf, n_lat // tb, tb=tb, final_norm=False)
        out = x
    return out
```

```python
import functools

import jax
import jax.numpy as jnp
from jax import lax
from jax.experimental import pallas as pl
from jax.experimental.pallas import tpu as pltpu

F32 = jnp.float32
BF16 = jnp.bfloat16
I32 = jnp.int32

D_MODEL = 1024
GRID_W = 64
EPS = 1e-6
HEAD_DIM = 64
ATTN_W = 512
N_Q_HEADS = 8
N_KV_HEADS = 2
Q_GROUP = 4
KV_W = 128
ROPE_THETA = 10000.0
AXIS_DIM = 32
CONV_W = 256
CONV_K = 31
CONV_HALO = 16
CHUNK_W = 256
CHUNK_HEADS = 4
CHUNK = 128
Q0, K0, V0, CV0, CH0, IN_W = 0, 512, 640, 768, 1280, 1792
N_EXPERTS = 32
TOP_K = 4
SWIGLU_LIMIT = 7.0
SWIGLU_ALPHA = 1.702

LANES = 128
SUBLANES = 8
MXU_COLS = 256
VMEM_LIMIT = 56 * 1024 * 1024

ROW_TILE = 512
ATTN_Q_TILE = 256
ATTN_K_TILE = 512
CONV_TILE = 512
CONV_ROWS = 64
ROUTER_TILE = 512
DISPATCH_TILE = 1024
EXPERT_TILE = 512
COMBINE_TILE = 256


def _fit(n, pref):
    return max(t for t in range(SUBLANES, min(n, pref) + 1, SUBLANES) if n % t == 0)


def _cparams(*sem):
    return pltpu.CompilerParams(dimension_semantics=sem, vmem_limit_bytes=VMEM_LIMIT)


def _split_bf16(a):
    hi = a.astype(BF16)
    lo = (a - hi.astype(F32)).astype(BF16)
    return hi, lo


def _dot(a, b):
    return jnp.dot(a, b, preferred_element_type=F32)


def _dot_nt(a, b):
    return lax.dot_general(a, b, (((1,), (1,)), ((), ())), preferred_element_type=F32)


def _dot3(a, b, dot=_dot):
    ah, al = _split_bf16(a)
    bh, bl = _split_bf16(b)
    return dot(ah, bh) + (dot(ah, bl) + dot(al, bh))


def _rmsnorm(x, g):
    return x * lax.rsqrt(jnp.mean(x * x, axis=-1, keepdims=True) + EPS) * g


def _layernorm(x, g, b):
    mu = jnp.mean(x, axis=-1, keepdims=True)
    xc = x - mu
    return xc * lax.rsqrt(jnp.mean(xc * xc, axis=-1, keepdims=True) + EPS) * g + b


def _ada_kernel(c_ref, w_ref, b_ref, o_ref):
    c = c_ref[...]
    s = c * jax.nn.sigmoid(c)
    o_ref[...] = _dot3(s, w_ref[...]) + b_ref[...]


def _ada(cs, w_ada, b_ada):
    depth, d, n6 = w_ada.shape
    r = cs.shape[0]
    tn = 1536
    return pl.pallas_call(
        _ada_kernel,
        out_shape=jax.ShapeDtypeStruct((depth, r, n6), F32),
        grid=(depth, n6 // tn),
        in_specs=[
            pl.BlockSpec((r, d), lambda l, j: (0, 0)),
            pl.BlockSpec((None, d, tn), lambda l, j: (l, 0, j)),
            pl.BlockSpec((None, 1, tn), lambda l, j: (l, 0, j)),
        ],
        out_specs=pl.BlockSpec((None, r, tn), lambda l, j: (l, 0, j)),
        compiler_params=_cparams("arbitrary", "arbitrary"),
        name="ada",
    )(cs, w_ada, b_ada.reshape(depth, 1, n6))


def _head_sumsq(t, bd):
    hi, lo = _split_bf16(t * t)
    return _dot(hi, bd) + _dot(lo, bd)


def _rope(t, cos, sin):
    w = t.shape[1]
    reps = w // LANES
    cosw = jnp.concatenate([cos] * reps, axis=1) if reps > 1 else cos
    sinw = jnp.concatenate([sin] * reps, axis=1) if reps > 1 else sin
    lane = lax.broadcasted_iota(I32, t.shape, 1)
    first = (lane % HEAD_DIM) < (HEAD_DIM // 2)
    rot = jnp.where(first, pltpu.roll(t, w - HEAD_DIM // 2, 1), pltpu.roll(t, HEAD_DIM // 2, 1))
    return t * cosw + rot * sinw


def _modulated(x_ref, sh_ref, sc_ref, g_ref):
    return _rmsnorm(x_ref[...], g_ref[...]) * (1.0 + sc_ref[...]) + sh_ref[...]


def _qk_head(t, g, bd, cos_ref, sin_ref, rope):
    t = t * lax.rsqrt(_head_sumsq(t, bd) * (1.0 / HEAD_DIM) + EPS) * g
    if rope:
        t = _rope(t, cos_ref[...], sin_ref[...])
    return t


def _proj_kernel(x_ref, sh_ref, sc_ref, g1_ref, w_ref, gq_ref, gk_ref, cos_ref, sin_ref, bd_ref,
                 q_ref, kt_ref, v_ref, zc_ref, zs_ref, *, rope):
    h = _modulated(x_ref, sh_ref, sc_ref, g1_ref)
    p = _dot(h.astype(BF16), w_ref[...])
    q = _qk_head(p[:, Q0:K0], gq_ref[...], bd_ref[...], cos_ref, sin_ref, rope)
    q_ref[...] = (q * (HEAD_DIM ** -0.5)).astype(BF16)
    k = _qk_head(p[:, K0:V0], gk_ref[...], bd_ref[0:KV_W, 0:KV_W], cos_ref, sin_ref, rope)
    kt_ref[...] = k.T.astype(BF16)
    v_ref[...] = p[:, V0:CV0].astype(BF16)
    zc_ref[...] = p[:, CV0:CH0]
    zs_ref[...] = p[:, CH0:IN_W]


def _proj_kv_kernel(x_ref, sh_ref, sc_ref, g1_ref, w_ref, gk_ref, bd_ref, kt_ref, v_ref):
    h = _modulated(x_ref, sh_ref, sc_ref, g1_ref)
    p = _dot(h.astype(BF16), w_ref[...])
    k = _qk_head(p[:, 0:KV_W], gk_ref[...], bd_ref[0:KV_W, 0:KV_W], None, None, False)
    kt_ref[...] = k.T.astype(BF16)
    v_ref[...] = p[:, KV_W:2 * KV_W].astype(BF16)


def _mod_spec(layer, row_of, chunk):
    return pl.BlockSpec((None, None, 1, D_MODEL), lambda b, j: (layer, row_of(b), 0, chunk))


def _const_spec(shape):
    return pl.BlockSpec(shape, lambda b, j: tuple(0 for _ in shape))


def _proj(x, mod4, layer, row_of, g1, w_in_b, gq, gk, cos2, sin2, bd, *, rope, tm):
    bsz, s, d = x.shape
    grid = (bsz, s // tm)
    row = lambda w: pl.BlockSpec((None, tm, w), lambda b, j: (b, j, 0))
    return pl.pallas_call(
        functools.partial(_proj_kernel, rope=rope),
        out_shape=(
            jax.ShapeDtypeStruct((bsz, s, ATTN_W), BF16),
            jax.ShapeDtypeStruct((bsz, KV_W, s), BF16),
            jax.ShapeDtypeStruct((bsz, s, KV_W), BF16),
            jax.ShapeDtypeStruct((bsz, s, 2 * CONV_W), F32),
            jax.ShapeDtypeStruct((bsz, s, 2 * CHUNK_W), F32),
        ),
        grid=grid,
        in_specs=[
            row(d),
            _mod_spec(layer, row_of, 0),
            _mod_spec(layer, row_of, 1),
            _const_spec((1, d)),
            _const_spec((d, IN_W)),
            _const_spec((1, ATTN_W)),
            _const_spec((1, KV_W)),
            pl.BlockSpec((tm, LANES), lambda b, j: (j, 0)),
            pl.BlockSpec((tm, LANES), lambda b, j: (j, 0)),
            _const_spec((ATTN_W, ATTN_W)),
        ],
        out_specs=(
            row(ATTN_W),
            pl.BlockSpec((None, KV_W, tm), lambda b, j: (b, 0, j)),
            row(KV_W),
            row(2 * CONV_W),
            row(2 * CHUNK_W),
        ),
        compiler_params=_cparams("arbitrary", "arbitrary"),
        name="proj",
    )(x, mod4, mod4, g1, w_in_b, gq, gk, cos2, sin2, bd)


def _proj_kv(x, mod4, layer, row_of, g1, w_kv_b, gk, bd, *, tm):
    bsz, s, d = x.shape
    return pl.pallas_call(
        _proj_kv_kernel,
        out_shape=(
            jax.ShapeDtypeStruct((bsz, KV_W, s), BF16),
            jax.ShapeDtypeStruct((bsz, s, KV_W), BF16),
        ),
        grid=(bsz, s // tm),
        in_specs=[
            pl.BlockSpec((None, tm, d), lambda b, j: (b, j, 0)),
            _mod_spec(layer, row_of, 0),
            _mod_spec(layer, row_of, 1),
            _const_spec((1, d)),
            _const_spec((d, 2 * KV_W)),
            _const_spec((1, KV_W)),
            _const_spec((ATTN_W, ATTN_W)),
        ],
        out_specs=(
            pl.BlockSpec((None, KV_W, tm), lambda b, j: (b, 0, j)),
            pl.BlockSpec((None, tm, KV_W), lambda b, j: (b, j, 0)),
        ),
        compiler_params=_cparams("arbitrary", "arbitrary"),
        name="proj_kv",
    )(x, mod4, mod4, g1, w_kv_b, gk, bd)


def _attn_kernel(*refs, n_src):
    q_ref = refs[0]
    kt_refs = refs[1:1 + n_src]
    v_refs = refs[1 + n_src:1 + 2 * n_src]
    o_ref, s_ref, p_ref = refs[1 + 2 * n_src:]
    tiles, spans, col = [], [], 0
    for i, kt in enumerate(kt_refs):
        kn = kt.shape[1]
        tk = next(w for w in (ATTN_K_TILE, MXU_COLS, LANES) if kn % w == 0)
        tiles += [(i, o, col + o, tk) for o in range(0, kn, tk)]
        spans.append((col, kn))
        col += kn

    def scores_into(slot, h):
        g = h // Q_GROUP
        qh = q_ref[:, h * HEAD_DIM:(h + 1) * HEAD_DIM]
        mx = None
        for i, o, c, tk in tiles:
            tile = _dot(qh, kt_refs[i][g * HEAD_DIM:(g + 1) * HEAD_DIM, o:o + tk])
            s_ref[slot, :, c:c + tk] = tile
            for lo in range(0, tk, LANES):
                mx = tile[:, lo:lo + LANES] if mx is None else jnp.maximum(mx, tile[:, lo:lo + LANES])
        return jnp.max(mx, axis=-1, keepdims=True)

    m_next = scores_into(0, 0)
    outs = []
    for h in range(N_Q_HEADS):
        g, slot, m = h // Q_GROUP, h % 2, m_next
        if h + 1 < N_Q_HEADS:
            m_next = scores_into(1 - slot, h + 1)
        p = jnp.exp(s_ref[slot] - m)
        denom = jnp.sum(p, axis=-1, keepdims=True)
        p_ref[slot] = p.astype(BF16)
        o = functools.reduce(jnp.add, [_dot(p_ref[slot, :, c:c + kn], v[...]) for (c, kn), v in zip(spans, v_refs)])
        outs.append(o[:, g * HEAD_DIM:(g + 1) * HEAD_DIM] / denom)
    o_ref[...] = jnp.concatenate(outs, axis=1).astype(BF16)


def _attention(q, kts, vs, *, tq):
    bsz, nq, _ = q.shape
    n_src = len(kts)
    kn_all = sum(kt.shape[2] for kt in kts)
    in_specs = [pl.BlockSpec((None, tq, ATTN_W), lambda b, j: (b, j, 0))]
    in_specs += [pl.BlockSpec((None, KV_W, kt.shape[2]), lambda b, j: (b, 0, 0)) for kt in kts]
    in_specs += [pl.BlockSpec((None, v.shape[1], KV_W), lambda b, j: (b, 0, 0)) for v in vs]
    return pl.pallas_call(
        functools.partial(_attn_kernel, n_src=n_src),
        out_shape=jax.ShapeDtypeStruct((bsz, nq, ATTN_W), BF16),
        grid=(bsz, nq // tq),
        in_specs=in_specs,
        out_specs=pl.BlockSpec((None, tq, ATTN_W), lambda b, j: (b, j, 0)),
        scratch_shapes=[pltpu.VMEM((2, tq, kn_all), F32), pltpu.VMEM((2, tq, kn_all), BF16)],
        compiler_params=_cparams("arbitrary", "arbitrary"),
        name="attention",
    )(q, *kts, *vs)


def _glu(z):
    return z[:, :CONV_W] * jax.nn.sigmoid(z[:, CONV_W:])


def _conv_kernel(zp_ref, z_ref, zn_ref, w_ref, bdw_ref, g_ref, b_ref, o_ref, u_ref, *, ts):
    j = pl.program_id(1)
    nj = pl.num_programs(1)
    u_ref[0:CONV_HALO, :] = jnp.where(j > 0, _glu(zp_ref[...]), 0.0)
    u_ref[CONV_HALO + ts:CONV_HALO + ts + CONV_HALO, :] = jnp.where(j < nj - 1, _glu(zn_ref[...]), 0.0)
    u_ref[CONV_HALO:CONV_HALO + ts, :] = _glu(z_ref[...])
    off = CONV_HALO - CONV_K // 2
    for c in range(ts // CONV_ROWS):
        base = c * CONV_ROWS + off
        acc = jnp.zeros((CONV_ROWS, CONV_W), F32)
        for k in range(CONV_K):
            acc = acc + u_ref[base + k:base + k + CONV_ROWS, :] * w_ref[k:k + 1, :]
        y = _layernorm(acc + bdw_ref[...], g_ref[...], b_ref[...])
        o_ref[c * CONV_ROWS:(c + 1) * CONV_ROWS, :] = (y * jax.nn.sigmoid(y)).astype(BF16)


def _conv(zc, w_dw, b_dw, g_ln, b_ln, *, ts):
    bsz, s, _ = zc.shape
    nh = ts // CONV_HALO
    last = s // CONV_HALO - 1
    return pl.pallas_call(
        functools.partial(_conv_kernel, ts=ts),
        out_shape=jax.ShapeDtypeStruct((bsz, s, CONV_W), BF16),
        grid=(bsz, s // ts),
        in_specs=[
            pl.BlockSpec((None, CONV_HALO, 2 * CONV_W), lambda b, j: (b, jnp.maximum(j * nh - 1, 0), 0)),
            pl.BlockSpec((None, ts, 2 * CONV_W), lambda b, j: (b, j, 0)),
            pl.BlockSpec((None, CONV_HALO, 2 * CONV_W), lambda b, j: (b, jnp.minimum((j + 1) * nh, last), 0)),
            _const_spec((CONV_K, CONV_W)),
            _const_spec((1, CONV_W)),
            _const_spec((1, CONV_W)),
            _const_spec((1, CONV_W)),
        ],
        out_specs=pl.BlockSpec((None, ts, CONV_W), lambda b, j: (b, j, 0)),
        scratch_shapes=[pltpu.VMEM((ts + 2 * CONV_HALO, CONV_W), F32)],
        compiler_params=_cparams("arbitrary", "arbitrary"),
        name="conv",
    )(zc, zc, zc, w_dw, b_dw, g_ln, b_ln)


def _sgu_kernel(z_ref, g_ref, b_ref, ws_ref, bs_ref, o_ref, *, tm):
    z = z_ref[...]
    z = 0.5 * z * (1.0 + lax.erf(z * (2.0 ** -0.5)))
    u = z[:, :CHUNK_W]
    v = _layernorm(z[:, CHUNK_W:], g_ref[...], b_ref[...]).astype(BF16)
    head = lax.broadcasted_iota(I32, (CHUNK, CHUNK_W), 1) // (CHUNK_W // CHUNK_HEADS)
    for c in range(tm // CHUNK):
        vc = v[c * CHUNK:(c + 1) * CHUNK, :]
        s = bs_ref[...]
        for h in range(CHUNK_HEADS):
            s = s + jnp.where(head == h, _dot(ws_ref[h], vc), 0.0)
        o_ref[c * CHUNK:(c + 1) * CHUNK, :] = (u[c * CHUNK:(c + 1) * CHUNK, :] * s).astype(BF16)


def _sgu(zs, g_ln, b_ln, ws_b, bs_full, *, tm):
    bsz, s, _ = zs.shape
    return pl.pallas_call(
        functools.partial(_sgu_kernel, tm=tm),
        out_shape=jax.ShapeDtypeStruct((bsz, s, CHUNK_W), BF16),
        grid=(bsz, s // tm),
        in_specs=[
            pl.BlockSpec((None, tm, 2 * CHUNK_W), lambda b, j: (b, j, 0)),
            _const_spec((1, CHUNK_W)),
            _const_spec((1, CHUNK_W)),
            _const_spec((CHUNK_HEADS, CHUNK, CHUNK)),
            _const_spec((CHUNK, CHUNK_W)),
        ],
        out_specs=pl.BlockSpec((None, tm, CHUNK_W), lambda b, j: (b, j, 0)),
        compiler_params=_cparams("arbitrary", "arbitrary"),
        name="sgu",
    )(zs, g_ln, b_ln, ws_b, bs_full)


def _oproj_kernel(a_ref, cv_ref, ch_ref, x_ref, gt_ref, sh_ref, sc_ref, g2_ref, wo_ref, xo_ref, h2_ref):
    y = (_dot(a_ref[...], wo_ref[0:ATTN_W, :])
         + _dot(cv_ref[...], wo_ref[ATTN_W:ATTN_W + CONV_W, :])
         + _dot(ch_ref[...], wo_ref[ATTN_W + CONV_W:, :]))
    xn = x_ref[...] + gt_ref[...] * y
    xo_ref[...] = xn
    h2_ref[...] = _rmsnorm(xn, g2_ref[...]) * (1.0 + sc_ref[...]) + sh_ref[...]


def _oproj(a, cv, ch, x, mod4, layer, row_of, g2, wo_b, *, tm):
    bsz, s, d = x.shape
    row = lambda w: pl.BlockSpec((None, tm, w), lambda b, j: (b, j, 0))
    return pl.pallas_call(
        _oproj_kernel,
        out_shape=(jax.ShapeDtypeStruct((bsz, s, d), F32), jax.ShapeDtypeStruct((bsz, s, d), F32)),
        grid=(bsz, s // tm),
        in_specs=[
            row(ATTN_W), row(CONV_W), row(CHUNK_W), row(d),
            _mod_spec(layer, row_of, 2),
            _mod_spec(layer, row_of, 3),
            _mod_spec(layer, row_of, 4),
            _const_spec((1, d)),
            _const_spec((d, d)),
        ],
        out_specs=(row(d), row(d)),
        compiler_params=_cparams("arbitrary", "arbitrary"),
        name="oproj",
    )(a, cv, ch, x, mod4, mod4, mod4, g2, wo_b)


def _router_kernel(h_ref, wr_ref, br_ref, idx_ref, wt_ref, rank_ref, cnt_ref, carry_ref, *, tb):
    @pl.when(pl.program_id(0) == 0)
    def _():
        carry_ref[...] = jnp.zeros_like(carry_ref)

    logits = _dot3(wr_ref[...], h_ref[...], dot=_dot_nt) + br_ref[...]
    e_iota = lax.broadcasted_iota(I32, (N_EXPERTS, tb), 0)
    vals, idxs = [], []
    work = logits
    for _ in range(TOP_K):
        m = jnp.max(work, axis=0, keepdims=True)
        ik = jnp.min(jnp.where(work == m, e_iota, N_EXPERTS), axis=0, keepdims=True)
        vals.append(m)
        idxs.append(ik)
        work = jnp.where(e_iota == ik, -jnp.inf, work)
    exps = [jnp.exp(v - vals[0]) for v in vals]
    denom = functools.reduce(jnp.add, exps)
    wt_ref[...] = jnp.concatenate([e / denom for e in exps], axis=0)
    idx_ref[...] = jnp.concatenate(idxs, axis=0)

    onehot = functools.reduce(jnp.add, [(e_iota == ik).astype(F32) for ik in idxs])
    upper = (lax.broadcasted_iota(I32, (tb, tb), 0) < lax.broadcasted_iota(I32, (tb, tb), 1)).astype(BF16)
    before = _dot(onehot.astype(BF16), upper) + carry_ref[...]
    ranks = [jnp.sum(jnp.where(e_iota == ik, before, 0.0), axis=0, keepdims=True) for ik in idxs]
    rank_ref[...] = jnp.concatenate(ranks, axis=0).astype(I32)
    carry_ref[...] = carry_ref[...] + jnp.sum(onehot, axis=1, keepdims=True)
    cnt_ref[...] = jnp.broadcast_to(carry_ref[...], cnt_ref.shape)


def _router(h2, wr_t, br, *, tb):
    n, d = h2.shape
    tok = lambda dt: jax.ShapeDtypeStruct((TOP_K, n), dt)
    tok_spec = pl.BlockSpec((TOP_K, tb), lambda i: (0, i))
    return pl.pallas_call(
        functools.partial(_router_kernel, tb=tb),
        out_shape=(tok(I32), tok(F32), tok(I32), jax.ShapeDtypeStruct((N_EXPERTS, LANES), F32)),
        grid=(n // tb,),
        in_specs=[
            pl.BlockSpec((tb, d), lambda i: (i, 0)),
            pl.BlockSpec((N_EXPERTS, d), lambda i: (0, 0)),
            pl.BlockSpec((N_EXPERTS, 1), lambda i: (0, 0)),
        ],
        out_specs=(tok_spec, tok_spec, tok_spec, pl.BlockSpec((N_EXPERTS, LANES), lambda i: (0, 0))),
        scratch_shapes=[pltpu.VMEM((N_EXPERTS, 1), F32)],
        compiler_params=_cparams("arbitrary"),
        name="router",
    )(h2, wr_t, br)


def _pos_kernel(start_ref, idx_ref, rank_ref, pos_ref):
    idx = idx_ref[...]
    pos = rank_ref[...]
    for e in range(N_EXPERTS):
        pos = pos + jnp.where(idx == e, start_ref[e], 0)
    pos_ref[...] = pos


def _positions(starts, idx, rank):
    return pl.pallas_call(
        _pos_kernel,
        out_shape=jax.ShapeDtypeStruct(idx.shape, I32),
        grid_spec=pltpu.PrefetchScalarGridSpec(
            num_scalar_prefetch=1, grid=(1,),
            in_specs=[pl.BlockSpec(idx.shape, lambda i, s: (0, 0)), pl.BlockSpec(idx.shape, lambda i, s: (0, 0))],
            out_specs=pl.BlockSpec(idx.shape, lambda i, s: (0, 0))),
        compiler_params=_cparams("arbitrary"),
        name="positions",
    )(starts, idx, rank)


def _row_copy(src, src_row, dst, dst_row, sem):
    return pltpu.make_async_copy(src.at[pl.ds(src_row, 1)], dst.at[pl.ds(dst_row, 1)], sem)


def _dispatch_kernel(zf_ref, pos_ref, h_ref, xs_ref, zbuf_ref, sem, zsem, *, tb, tm, nblk):
    @pl.when(pl.program_id(0) == 0)
    def _():
        zbuf_ref[...] = jnp.zeros_like(zbuf_ref)

        def zero_copy(i):
            return pltpu.make_async_copy(zbuf_ref, xs_ref.at[pl.ds(pl.multiple_of(i * tm, tm), tm)], zsem)

        def start(i, carry):
            @pl.when(zf_ref[i] != 0)
            def _():
                zero_copy(i).start()
            return carry

        def wait(i, carry):
            @pl.when(zf_ref[i] != 0)
            def _():
                zero_copy(i).wait()
            return carry

        lax.fori_loop(0, nblk, start, 0)
        lax.fori_loop(0, nblk, wait, 0)

    def issue(t8, carry):
        t0 = pl.multiple_of(t8 * SUBLANES, SUBLANES)
        for r in range(SUBLANES):
            for k in range(TOP_K):
                _row_copy(h_ref, t0 + r, xs_ref, pos_ref[0, k * tb + t0 + r], sem).start(priority=(r + k) % 2)
        return carry

    lax.fori_loop(0, tb // SUBLANES, issue, 0)
    pltpu.make_async_copy(xs_ref.at[pl.ds(0, TOP_K * tb)], xs_ref.at[pl.ds(0, TOP_K * tb)], sem).wait()


def _dispatch(zero_flag, pos_blocks, h2, *, tb, tm):
    n, d = h2.shape
    nblk = zero_flag.shape[0]
    return pl.pallas_call(
        functools.partial(_dispatch_kernel, tb=tb, tm=tm, nblk=nblk),
        out_shape=jax.ShapeDtypeStruct((nblk * tm, d), F32),
        grid_spec=pltpu.PrefetchScalarGridSpec(
            num_scalar_prefetch=1, grid=(n // tb,),
            in_specs=[
                pl.BlockSpec((None, 1, TOP_K * tb), lambda i, zf: (i, 0, 0), memory_space=pltpu.SMEM),
                pl.BlockSpec((tb, d), lambda i, zf: (i, 0)),
            ],
            out_specs=pl.BlockSpec(memory_space=pl.ANY),
            scratch_shapes=[pltpu.VMEM((tm, d), F32), pltpu.SemaphoreType.DMA(()), pltpu.SemaphoreType.DMA(())]),
        compiler_params=_cparams("arbitrary"),
        name="dispatch",
    )(zero_flag, pos_blocks, h2)


def _expert_kernel(be_ref, nb_ref, xs_ref, wg_ref, bg_ref, wu_ref, bu_ref, wd_ref, bd_ref, ys_ref, wb_ref):
    i = pl.program_id(0)
    used = i < nb_ref[0]
    new_expert = jnp.logical_or(i == 0, be_ref[i] != be_ref[jnp.maximum(i - 1, 0)])

    @pl.when(jnp.logical_and(used, new_expert))
    def _():
        wb_ref[0] = wg_ref[...].astype(BF16)
        wb_ref[1] = wu_ref[...].astype(BF16)
        wb_ref[2] = wd_ref[...].astype(BF16)

    @pl.when(used)
    def _():
        x = xs_ref[...].astype(BF16)
        g = jnp.minimum(_dot(x, wb_ref[0]) + bg_ref[...], SWIGLU_LIMIT)
        u = jnp.clip(_dot(x, wb_ref[1]) + bu_ref[...], -SWIGLU_LIMIT, SWIGLU_LIMIT)
        a = (u + 1.0) * (g * jax.nn.sigmoid(SWIGLU_ALPHA * g))
        ys_ref[...] = _dot(a.astype(BF16), wb_ref[2]) + bd_ref[...]

    @pl.when(jnp.logical_not(used))
    def _():
        ys_ref[...] = jnp.zeros_like(ys_ref)


def _experts(block_e, n_used, xs, layer, wg, bg, wu, bu, wd, bd, *, tm):
    p, d = xs.shape
    assert wg.shape[2:] == (d, d) and wd.shape[2:] == (d, d)
    wspec = lambda a: pl.BlockSpec((None, None, a, d), lambda i, be, nb: (layer, be[i], 0, 0))
    return pl.pallas_call(
        _expert_kernel,
        out_shape=jax.ShapeDtypeStruct((p, d), F32),
        grid_spec=pltpu.PrefetchScalarGridSpec(
            num_scalar_prefetch=2, grid=(p // tm,),
            in_specs=[
                pl.BlockSpec((tm, d), lambda i, be, nb: (i, 0)),
                wspec(d), wspec(1), wspec(d), wspec(1), wspec(d), wspec(1),
            ],
            out_specs=pl.BlockSpec((tm, d), lambda i, be, nb: (i, 0)),
            scratch_shapes=[pltpu.VMEM((3, d, d), BF16)]),
        compiler_params=_cparams("arbitrary"),
        name="experts",
    )(block_e, n_used, xs, wg, bg, wu, bu, wd, bd)


def _combine_kernel(pos_ref, posn_ref, ys_ref, wt_ref, x_ref, gt_ref, gf_ref, o_ref, buf_ref, sems, *, tb, final_norm):
    i = pl.program_id(0)
    n = pl.num_programs(0)

    def gather(p_ref, slot):
        def issue(t8, carry):
            t0 = pl.multiple_of(t8 * SUBLANES, SUBLANES)
            for r in range(SUBLANES):
                for k in range(TOP_K):
                    _row_copy(ys_ref, p_ref[0, k * tb + t0 + r], buf_ref.at[slot, k], t0 + r,
                              sems.at[slot]).start(priority=(r + k) % 2)
            return carry

        lax.fori_loop(0, tb // SUBLANES, issue, 0)

    @pl.when(i == 0)
    def _():
        gather(pos_ref, 0)

    @pl.when(i + 1 < n)
    def _():
        gather(posn_ref, (i + 1) % 2)

    slot = i % 2
    for k in range(TOP_K):
        pltpu.make_async_copy(ys_ref.at[pl.ds(0, tb)], buf_ref.at[slot, k], sems.at[slot]).wait()
    m = functools.reduce(jnp.add, [buf_ref[slot, k] * wt_ref[:, k:k + 1] for k in range(TOP_K)])
    xn = x_ref[...] + gt_ref[...] * m
    o_ref[...] = _rmsnorm(xn, gf_ref[...]) if final_norm else xn


def _combine(pos_blocks, ys, wt_tok, x, mod4, layer, row_of, g_final, blk0, *, tb, final_norm):
    bsz, s, d = x.shape
    nj = s // tb
    steps = bsz * nj
    return pl.pallas_call(
        functools.partial(_combine_kernel, tb=tb, final_norm=final_norm),
        out_shape=jax.ShapeDtypeStruct((bsz, s, d), F32),
        grid=(steps,),
        in_specs=[
            pl.BlockSpec((None, 1, TOP_K * tb), lambda i: (blk0 + i, 0, 0), memory_space=pltpu.SMEM),
            pl.BlockSpec((None, 1, TOP_K * tb), lambda i: (blk0 + jnp.minimum(i + 1, steps - 1), 0, 0),
                         memory_space=pltpu.SMEM),
            pl.BlockSpec(memory_space=pl.ANY),
            pl.BlockSpec((tb, TOP_K), lambda i: (blk0 + i, 0)),
            pl.BlockSpec((None, tb, d), lambda i: (i // nj, i % nj, 0)),
            pl.BlockSpec((None, None, 1, d), lambda i: (layer, row_of(i // nj), 0, 5)),
            pl.BlockSpec((1, d), lambda i: (0, 0)),
        ],
        out_specs=pl.BlockSpec((None, tb, d), lambda i: (i // nj, i % nj, 0)),
        scratch_shapes=[pltpu.VMEM((2, TOP_K, tb, d), F32), pltpu.SemaphoreType.DMA((2,))],
        compiler_params=_cparams("arbitrary"),
        name="combine",
    )(pos_blocks, pos_blocks, ys, wt_tok, x, mod4, g_final)


def _block_layout(pos, tb):
    n = pos.shape[1]
    return pos.reshape(TOP_K, n // tb, tb).transpose(1, 0, 2).reshape(n // tb, 1, TOP_K * tb)


def _moe(h2, wr_t, br, layer, wg, bg, wu, bu, wd, bd):
    n = h2.shape[0]
    tm = EXPERT_TILE
    td = _fit(n, DISPATCH_TILE)
    idx, wt, rank, cnt = _router(h2, wr_t, br, tb=_fit(n, ROUTER_TILE))
    counts = cnt[:, 0].astype(I32)
    padded = (counts + tm - 1) // tm * tm
    ends = jnp.cumsum(padded)
    pos = _positions((ends - padded).astype(I32), idx, rank)
    nblk = (n * TOP_K) // tm + N_EXPERTS
    n_used = ends[-1] // tm
    blk = jnp.arange(nblk, dtype=I32)
    first_row = jnp.minimum(blk, n_used - 1) * tm
    block_e = jnp.minimum(jnp.sum(ends[None, :] <= first_row[:, None], axis=1), N_EXPERTS - 1).astype(I32)
    partial_or_unused = (blk >= n_used) | jnp.any(((blk + 1) * tm)[:, None] == ends[None, :], axis=1)
    xs = _dispatch(partial_or_unused.astype(I32), _block_layout(pos, td), h2, tb=td, tm=tm)
    ys = _experts(block_e, n_used.reshape(1).astype(I32), xs, layer, wg, bg, wu, bu, wd, bd, tm=tm)
    return ys, pos, wt


def _rope_tables(s):
    pos = jnp.arange(s)
    r, col = (pos // GRID_W).astype(F32), (pos % GRID_W).astype(F32)
    inv = ROPE_THETA ** (-jnp.arange(0, AXIS_DIM, 2, dtype=F32) / AXIS_DIM)
    ang = jnp.concatenate([r[:, None] * inv, col[:, None] * inv], axis=-1)
    cos, sin = jnp.cos(ang), jnp.sin(ang)
    return jnp.tile(jnp.concatenate([cos, cos], axis=-1), (1, 2)), jnp.tile(jnp.concatenate([-sin, sin], axis=-1), (1, 2))


def kernel(x, c, ctx, c_ctx, w_ada, b_ada, g_norm1, w_in, g_q, g_k, w_dw, b_dw, g_conv_ln, b_conv_ln,
           g_sgu_ln, b_sgu_ln, w_s, b_s, w_o, g_norm2, w_router, b_router, w_gate, b_gate, w_up, b_up,
           w_down, b_down, g_final):
    bsz, s, d = x.shape
    cl = ctx.shape[1]
    depth = w_ada.shape[0]
    n_lat, n_ctx = bsz * s, bsz * cl
    ctx_row = 8 * ((bsz + 7) // 8)
    rows = ctx_row + 8

    cs = jnp.zeros((rows, d), F32).at[:bsz].set(c).at[ctx_row].set(c_ctx)
    mod4 = _ada(cs, w_ada, b_ada).reshape(depth, rows, 1, 6 * d)
    lat_row = lambda b: b
    ctx_row_of = lambda b: ctx_row

    cos2, sin2 = _rope_tables(s)
    cosc = jnp.zeros((cl, LANES), F32)
    head_of = jnp.arange(ATTN_W) // HEAD_DIM
    bd = (head_of[:, None] == head_of[None, :]).astype(BF16)

    tm_lat = min(ROW_TILE, s)
    tm_ctx = min(ROW_TILE, cl)
    xc = ctx
    out = None
    for l in range(depth):
        last = l == depth - 1
        g1 = g_norm1[l].reshape(1, d)
        w_in_b = w_in[l].astype(BF16)
        gq = jnp.tile(g_q[l], N_Q_HEADS).reshape(1, ATTN_W)
        gk = jnp.tile(g_k[l], N_KV_HEADS).reshape(1, KV_W)
        conv_args = (w_dw[l], b_dw[l].reshape(1, -1), g_conv_ln[l].reshape(1, -1), b_conv_ln[l].reshape(1, -1))
        sgu_args = (g_sgu_ln[l].reshape(1, -1), b_sgu_ln[l].reshape(1, -1), w_s[l].astype(BF16),
                    jnp.repeat(b_s[l].T, CHUNK_W // CHUNK_HEADS, axis=1))
        wo_b = w_o[l].astype(BF16)
        g2 = g_norm2[l].reshape(1, d)

        q, kt, v, zc, zs = _proj(x, mod4, l, lat_row, g1, w_in_b, gq, gk, cos2, sin2, bd, rope=True, tm=tm_lat)
        if last:
            ktc, vc = _proj_kv(xc, mod4, l, ctx_row_of, g1, w_in_b[:, K0:CV0], gk, bd, tm=tm_ctx)
        else:
            qc, ktc, vc, zcc, zsc = _proj(xc, mod4, l, ctx_row_of, g1, w_in_b, gq, gk, cosc, cosc, bd,
                                          rope=False, tm=tm_ctx)
        a = _attention(q, [kt, ktc], [v, vc], tq=min(ATTN_Q_TILE, s))
        cv = _conv(zc, *conv_args, ts=min(CONV_TILE, s))
        ch = _sgu(zs, *sgu_args, tm=tm_lat)
        x, h2 = _oproj(a, cv, ch, x, mod4, l, lat_row, g2, wo_b, tm=tm_lat)
        tokens = h2.reshape(n_lat, d)
        if not last:
            ac = _attention(qc, [ktc], [vc], tq=min(ATTN_Q_TILE, cl))
            cvc = _conv(zcc, *conv_args, ts=min(CONV_TILE, cl))
            chc = _sgu(zsc, *sgu_args, tm=tm_ctx)
            xc, h2c = _oproj(ac, cvc, chc, xc, mod4, l, ctx_row_of, g2, wo_b, tm=tm_ctx)
            tokens = jnp.concatenate([tokens, h2c.reshape(n_ctx, d)], axis=0)

        ys, pos, wt = _moe(tokens, w_router[l].T, b_router[l].reshape(N_EXPERTS, 1), l,
                           w_gate, b_gate.reshape(depth, N_EXPERTS, 1, -1),
                           w_up, b_up.reshape(depth, N_EXPERTS, 1, -1),
                           w_down, b_down.reshape(depth, N_EXPERTS, 1, -1))
        tb = min(COMBINE_TILE, cl)
        pos_blocks = _block_layout(pos, tb)
        wt_tok = wt.T
        gf = g_final.reshape(1, d)
        x = _combine(pos_blocks, ys, wt_tok, x, mod4, l, lat_row, gf, 0, tb=tb, final_norm=last)
        if not last:
            xc = _combine(pos_blocks, ys, wt_tok, xc, mod4, l, ctx_row_of, gf, n_lat // tb, tb=tb, final_norm=False)
        out = x
    return out
```

```python
import functools

import jax
import jax.numpy as jnp
from jax import lax
from jax.experimental import pallas as pl
from jax.experimental.pallas import tpu as pltpu

F32 = jnp.float32
BF16 = jnp.bfloat16
I32 = jnp.int32

D_MODEL = 1024
GRID_W = 64
EPS = 1e-6
HEAD_DIM = 64
ATTN_W = 512
N_Q_HEADS = 8
N_KV_HEADS = 2
Q_GROUP = 4
KV_W = 128
ROPE_THETA = 10000.0
AXIS_DIM = 32
CONV_W = 256
CONV_K = 31
CONV_HALO = 16
CHUNK_W = 256
CHUNK_HEADS = 4
CHUNK = 128
Q0, K0, V0, CV0, CH0, IN_W = 0, 512, 640, 768, 1280, 1792
N_EXPERTS = 32
TOP_K = 4
SWIGLU_LIMIT = 7.0
SWIGLU_ALPHA = 1.702

LANES = 128
SUBLANES = 8
MXU_COLS = 256
VMEM_LIMIT = 56 * 1024 * 1024

ROW_TILE = 512
ATTN_Q_TILE = 256
ATTN_K_TILE = 512
CONV_TILE = 512
CONV_ROWS = 64
ROUTER_TILE = 512
DISPATCH_TILE = 1024
EXPERT_TILE = 512
COMBINE_TILE = 256


def _fit(n, pref):
    return max(t for t in range(SUBLANES, min(n, pref) + 1, SUBLANES) if n % t == 0)


def _cparams(*sem):
    return pltpu.CompilerParams(dimension_semantics=sem, vmem_limit_bytes=VMEM_LIMIT)


def _split_bf16(a):
    hi = a.astype(BF16)
    lo = (a - hi.astype(F32)).astype(BF16)
    return hi, lo


def _dot(a, b):
    return jnp.dot(a, b, preferred_element_type=F32)


def _dot_nt(a, b):
    return lax.dot_general(a, b, (((1,), (1,)), ((), ())), preferred_element_type=F32)


def _dot3(a, b, dot=_dot):
    ah, al = _split_bf16(a)
    bh, bl = _split_bf16(b)
    return dot(ah, bh) + (dot(ah, bl) + dot(al, bh))


def _rmsnorm(x, g):
    return x * lax.rsqrt(jnp.mean(x * x, axis=-1, keepdims=True) + EPS) * g


def _layernorm(x, g, b):
    mu = jnp.mean(x, axis=-1, keepdims=True)
    xc = x - mu
    return xc * lax.rsqrt(jnp.mean(xc * xc, axis=-1, keepdims=True) + EPS) * g + b


def _ada_kernel(c_ref, w_ref, b_ref, o_ref):
    c = c_ref[...]
    s = c * jax.nn.sigmoid(c)
    o_ref[...] = _dot3(s, w_ref[...]) + b_ref[...]


def _ada(cs, w_ada, b_ada):
    depth, d, n6 = w_ada.shape
    r = cs.shape[0]
    tn = 1536
    return pl.pallas_call(
        _ada_kernel,
        out_shape=jax.ShapeDtypeStruct((depth, r, n6), F32),
        grid=(depth, n6 // tn),
        in_specs=[
            pl.BlockSpec((r, d), lambda l, j: (0, 0)),
            pl.BlockSpec((None, d, tn), lambda l, j: (l, 0, j)),
            pl.BlockSpec((None, 1, tn), lambda l, j: (l, 0, j)),
        ],
        out_specs=pl.BlockSpec((None, r, tn), lambda l, j: (l, 0, j)),
        compiler_params=_cparams("arbitrary", "arbitrary"),
        name="ada",
    )(cs, w_ada, b_ada.reshape(depth, 1, n6))


def _head_sumsq(t, bd):
    hi, lo = _split_bf16(t * t)
    return _dot(hi, bd) + _dot(lo, bd)


def _rope(t, cos, sin):
    w = t.shape[1]
    reps = w // LANES
    cosw = jnp.concatenate([cos] * reps, axis=1) if reps > 1 else cos
    sinw = jnp.concatenate([sin] * reps, axis=1) if reps > 1 else sin
    lane = lax.broadcasted_iota(I32, t.shape, 1)
    first = (lane % HEAD_DIM) < (HEAD_DIM // 2)
    rot = jnp.where(first, pltpu.roll(t, w - HEAD_DIM // 2, 1), pltpu.roll(t, HEAD_DIM // 2, 1))
    return t * cosw + rot * sinw


def _modulated(x_ref, sh_ref, sc_ref, g_ref):
    return _rmsnorm(x_ref[...], g_ref[...]) * (1.0 + sc_ref[...]) + sh_ref[...]


def _qk_head(t, g, bd, cos_ref, sin_ref, rope):
    t = t * lax.rsqrt(_head_sumsq(t, bd) * (1.0 / HEAD_DIM) + EPS) * g
    if rope:
        t = _rope(t, cos_ref[...], sin_ref[...])
    return t


def _proj_kernel(x_ref, sh_ref, sc_ref, g1_ref, w_ref, gq_ref, gk_ref, cos_ref, sin_ref, bd_ref,
                 q_ref, kt_ref, v_ref, zc_ref, zs_ref, *, rope):
    h = _modulated(x_ref, sh_ref, sc_ref, g1_ref)
    p = _dot(h.astype(BF16), w_ref[...])
    q = _qk_head(p[:, Q0:K0], gq_ref[...], bd_ref[...], cos_ref, sin_ref, rope)
    q_ref[...] = (q * (HEAD_DIM ** -0.5)).astype(BF16)
    k = _qk_head(p[:, K0:V0], gk_ref[...], bd_ref[0:KV_W, 0:KV_W], cos_ref, sin_ref, rope)
    kt_ref[...] = k.T.astype(BF16)
    v_ref[...] = p[:, V0:CV0].astype(BF16)
    zc_ref[...] = p[:, CV0:CH0]
    zs_ref[...] = p[:, CH0:IN_W]


def _proj_kv_kernel(x_ref, sh_ref, sc_ref, g1_ref, w_ref, gk_ref, bd_ref, kt_ref, v_ref):
    h = _modulated(x_ref, sh_ref, sc_ref, g1_ref)
    p = _dot(h.astype(BF16), w_ref[...])
    k = _qk_head(p[:, 0:KV_W], gk_ref[...], bd_ref[0:KV_W, 0:KV_W], None, None, False)
    kt_ref[...] = k.T.astype(BF16)
    v_ref[...] = p[:, KV_W:2 * KV_W].astype(BF16)


def _mod_spec(layer, row_of, chunk):
    return pl.BlockSpec((None, None, 1, D_MODEL), lambda b, j: (layer, row_of(b), 0, chunk))


def _const_spec(shape):
    return pl.BlockSpec(shape, lambda b, j: tuple(0 for _ in shape))


def _proj(x, mod4, layer, row_of, g1, w_in_b, gq, gk, cos2, sin2, bd, *, rope, tm):
    bsz, s, d = x.shape
    grid = (bsz, s // tm)
    row = lambda w: pl.BlockSpec((None, tm, w), lambda b, j: (b, j, 0))
    return pl.pallas_call(
        functools.partial(_proj_kernel, rope=rope),
        out_shape=(
            jax.ShapeDtypeStruct((bsz, s, ATTN_W), BF16),
            jax.ShapeDtypeStruct((bsz, KV_W, s), BF16),
            jax.ShapeDtypeStruct((bsz, s, KV_W), BF16),
            jax.ShapeDtypeStruct((bsz, s, 2 * CONV_W), F32),
            jax.ShapeDtypeStruct((bsz, s, 2 * CHUNK_W), F32),
        ),
        grid=grid,
        in_specs=[
            row(d),
            _mod_spec(layer, row_of, 0),
            _mod_spec(layer, row_of, 1),
            _const_spec((1, d)),
            _const_spec((d, IN_W)),
            _const_spec((1, ATTN_W)),
            _const_spec((1, KV_W)),
            pl.BlockSpec((tm, LANES), lambda b, j: (j, 0)),
            pl.BlockSpec((tm, LANES), lambda b, j: (j, 0)),
            _const_spec((ATTN_W, ATTN_W)),
        ],
        out_specs=(
            row(ATTN_W),
            pl.BlockSpec((None, KV_W, tm), lambda b, j: (b, 0, j)),
            row(KV_W),
            row(2 * CONV_W),
            row(2 * CHUNK_W),
        ),
        compiler_params=_cparams("arbitrary", "arbitrary"),
        name="proj",
    )(x, mod4, mod4, g1, w_in_b, gq, gk, cos2, sin2, bd)


def _proj_kv(x, mod4, layer, row_of, g1, w_kv_b, gk, bd, *, tm):
    bsz, s, d = x.shape
    return pl.pallas_call(
        _proj_kv_kernel,
        out_shape=(
            jax.ShapeDtypeStruct((bsz, KV_W, s), BF16),
            jax.ShapeDtypeStruct((bsz, s, KV_W), BF16),
        ),
        grid=(bsz, s // tm),
        in_specs=[
            pl.BlockSpec((None, tm, d), lambda b, j: (b, j, 0)),
            _mod_spec(layer, row_of, 0),
            _mod_spec(layer, row_of, 1),
            _const_spec((1, d)),
            _const_spec((d, 2 * KV_W)),
            _const_spec((1, KV_W)),
            _const_spec((ATTN_W, ATTN_W)),
        ],
        out_specs=(
            pl.BlockSpec((None, KV_W, tm), lambda b, j: (b, 0, j)),
            pl.BlockSpec((None, tm, KV_W), lambda b, j: (b, j, 0)),
        ),
        compiler_params=_cparams("arbitrary", "arbitrary"),
        name="proj_kv",
    )(x, mod4, mod4, g1, w_kv_b, gk, bd)


def _attn_kernel(*refs, n_src):
    q_ref = refs[0]
    kt_refs = refs[1:1 + n_src]
    v_refs = refs[1 + n_src:1 + 2 * n_src]
    o_ref, s_ref, p_ref = refs[1 + 2 * n_src:]
    tiles, spans, col = [], [], 0
    for i, kt in enumerate(kt_refs):
        kn = kt.shape[1]
        tk = next(w for w in (ATTN_K_TILE, MXU_COLS, LANES) if kn % w == 0)
        tiles += [(i, o, col + o, tk) for o in range(0, kn, tk)]
        spans.append((col, kn))
        col += kn

    def scores_into(slot, h):
        g = h // Q_GROUP
        qh = q_ref[:, h * HEAD_DIM:(h + 1) * HEAD_DIM]
        mx = None
        for i, o, c, tk in tiles:
            tile = _dot(qh, kt_refs[i][g * HEAD_DIM:(g + 1) * HEAD_DIM, o:o + tk])
            s_ref[slot, :, c:c + tk] = tile
            for lo in range(0, tk, LANES):
                mx = tile[:, lo:lo + LANES] if mx is None else jnp.maximum(mx, tile[:, lo:lo + LANES])
        return jnp.max(mx, axis=-1, keepdims=True)

    m_next = scores_into(0, 0)
    outs = []
    for h in range(N_Q_HEADS):
        g, slot, m = h // Q_GROUP, h % 2, m_next
        if h + 1 < N_Q_HEADS:
            m_next = scores_into(1 - slot, h + 1)
        p = jnp.exp(s_ref[slot] - m)
        denom = jnp.sum(p, axis=-1, keepdims=True)
        p_ref[slot] = p.astype(BF16)
        o = functools.reduce(jnp.add, [_dot(p_ref[slot, :, c:c + kn], v[...]) for (c, kn), v in zip(spans, v_refs)])
        outs.append(o[:, g * HEAD_DIM:(g + 1) * HEAD_DIM] / denom)
    o_ref[...] = jnp.concatenate(outs, axis=1).astype(BF16)


def _attention(q, kts, vs, *, tq):
    bsz, nq, _ = q.shape
    n_src = len(kts)
    kn_all = sum(kt.shape[2] for kt in kts)
    in_specs = [pl.BlockSpec((None, tq, ATTN_W), lambda b, j: (b, j, 0))]
    in_specs += [pl.BlockSpec((None, KV_W, kt.shape[2]), lambda b, j: (b, 0, 0)) for kt in kts]
    in_specs += [pl.BlockSpec((None, v.shape[1], KV_W), lambda b, j: (b, 0, 0)) for v in vs]
    return pl.pallas_call(
        functools.partial(_attn_kernel, n_src=n_src),
        out_shape=jax.ShapeDtypeStruct((bsz, nq, ATTN_W), BF16),
        grid=(bsz, nq // tq),
        in_specs=in_specs,
        out_specs=pl.BlockSpec((None, tq, ATTN_W), lambda b, j: (b, j, 0)),
        scratch_shapes=[pltpu.VMEM((2, tq, kn_all), F32), pltpu.VMEM((2, tq, kn_all), BF16)],
        compiler_params=_cparams("arbitrary", "arbitrary"),
        name="attention",
    )(q, *kts, *vs)


def _glu(z):
    return z[:, :CONV_W] * jax.nn.sigmoid(z[:, CONV_W:])


def _conv_kernel(zp_ref, z_ref, zn_ref, w_ref, bdw_ref, g_ref, b_ref, o_ref, u_ref, us_ref, *, ts):
    j = pl.program_id(1)
    nj = pl.num_programs(1)
    u_ref[0:CONV_HALO, :] = jnp.where(j > 0, _glu(zp_ref[...]), 0.0)
    u_ref[CONV_HALO + ts:CONV_HALO + ts + CONV_HALO, :] = jnp.where(j < nj - 1, _glu(zn_ref[...]), 0.0)
    u_ref[CONV_HALO:CONV_HALO + ts, :] = _glu(z_ref[...])
    rows = us_ref.shape[1]
    for b in range(1, SUBLANES):
        us_ref[b - 1] = u_ref[b:b + rows, :]
    off = CONV_HALO - CONV_K // 2
    for c in range(ts // CONV_ROWS):
        acc = jnp.zeros((CONV_ROWS, CONV_W), F32)
        for k in range(CONV_K):
            start = c * CONV_ROWS + off + k
            b, lo = start % SUBLANES, start - start % SUBLANES
            window = u_ref[lo:lo + CONV_ROWS, :] if b == 0 else us_ref[b - 1, lo:lo + CONV_ROWS, :]
            acc = acc + window * w_ref[k:k + 1, :]
        y = _layernorm(acc + bdw_ref[...], g_ref[...], b_ref[...])
        o_ref[c * CONV_ROWS:(c + 1) * CONV_ROWS, :] = (y * jax.nn.sigmoid(y)).astype(BF16)


def _conv(zc, w_dw, b_dw, g_ln, b_ln, *, ts):
    bsz, s, _ = zc.shape
    nh = ts // CONV_HALO
    last = s // CONV_HALO - 1
    return pl.pallas_call(
        functools.partial(_conv_kernel, ts=ts),
        out_shape=jax.ShapeDtypeStruct((bsz, s, CONV_W), BF16),
        grid=(bsz, s // ts),
        in_specs=[
            pl.BlockSpec((None, CONV_HALO, 2 * CONV_W), lambda b, j: (b, jnp.maximum(j * nh - 1, 0), 0)),
            pl.BlockSpec((None, ts, 2 * CONV_W), lambda b, j: (b, j, 0)),
            pl.BlockSpec((None, CONV_HALO, 2 * CONV_W), lambda b, j: (b, jnp.minimum((j + 1) * nh, last), 0)),
            _const_spec((CONV_K, CONV_W)),
            _const_spec((1, CONV_W)),
            _const_spec((1, CONV_W)),
            _const_spec((1, CONV_W)),
        ],
        out_specs=pl.BlockSpec((None, ts, CONV_W), lambda b, j: (b, j, 0)),
        scratch_shapes=[pltpu.VMEM((ts + 2 * CONV_HALO, CONV_W), F32),
                        pltpu.VMEM((SUBLANES - 1, ts + 2 * CONV_HALO - SUBLANES, CONV_W), F32)],
        compiler_params=_cparams("arbitrary", "arbitrary"),
        name="conv",
    )(zc, zc, zc, w_dw, b_dw, g_ln, b_ln)


def _sgu_kernel(z_ref, g_ref, b_ref, ws_ref, bs_ref, o_ref, *, tm):
    z = z_ref[...]
    z = 0.5 * z * (1.0 + lax.erf(z * (2.0 ** -0.5)))
    u = z[:, :CHUNK_W]
    v = _layernorm(z[:, CHUNK_W:], g_ref[...], b_ref[...]).astype(BF16)
    head = lax.broadcasted_iota(I32, (CHUNK, CHUNK_W), 1) // (CHUNK_W // CHUNK_HEADS)
    for c in range(tm // CHUNK):
        vc = v[c * CHUNK:(c + 1) * CHUNK, :]
        s = bs_ref[...]
        for h in range(CHUNK_HEADS):
            s = s + jnp.where(head == h, _dot(ws_ref[h], vc), 0.0)
        o_ref[c * CHUNK:(c + 1) * CHUNK, :] = (u[c * CHUNK:(c + 1) * CHUNK, :] * s).astype(BF16)


def _sgu(zs, g_ln, b_ln, ws_b, bs_full, *, tm):
    bsz, s, _ = zs.shape
    return pl.pallas_call(
        functools.partial(_sgu_kernel, tm=tm),
        out_shape=jax.ShapeDtypeStruct((bsz, s, CHUNK_W), BF16),
        grid=(bsz, s // tm),
        in_specs=[
            pl.BlockSpec((None, tm, 2 * CHUNK_W), lambda b, j: (b, j, 0)),
            _const_spec((1, CHUNK_W)),
            _const_spec((1, CHUNK_W)),
            _const_spec((CHUNK_HEADS, CHUNK, CHUNK)),
            _const_spec((CHUNK, CHUNK_W)),
        ],
        out_specs=pl.BlockSpec((None, tm, CHUNK_W), lambda b, j: (b, j, 0)),
        compiler_params=_cparams("arbitrary", "arbitrary"),
        name="sgu",
    )(zs, g_ln, b_ln, ws_b, bs_full)


def _oproj_kernel(a_ref, cv_ref, ch_ref, x_ref, gt_ref, sh_ref, sc_ref, g2_ref, wo_ref, xo_ref, h2_ref):
    y = (_dot(a_ref[...], wo_ref[0:ATTN_W, :])
         + _dot(cv_ref[...], wo_ref[ATTN_W:ATTN_W + CONV_W, :])
         + _dot(ch_ref[...], wo_ref[ATTN_W + CONV_W:, :]))
    xn = x_ref[...] + gt_ref[...] * y
    xo_ref[...] = xn
    h2_ref[...] = _rmsnorm(xn, g2_ref[...]) * (1.0 + sc_ref[...]) + sh_ref[...]


def _oproj(a, cv, ch, x, mod4, layer, row_of, g2, wo_b, *, tm):
    bsz, s, d = x.shape
    row = lambda w: pl.BlockSpec((None, tm, w), lambda b, j: (b, j, 0))
    return pl.pallas_call(
        _oproj_kernel,
        out_shape=(jax.ShapeDtypeStruct((bsz, s, d), F32), jax.ShapeDtypeStruct((bsz, s, d), F32)),
        grid=(bsz, s // tm),
        in_specs=[
            row(ATTN_W), row(CONV_W), row(CHUNK_W), row(d),
            _mod_spec(layer, row_of, 2),
            _mod_spec(layer, row_of, 3),
            _mod_spec(layer, row_of, 4),
            _const_spec((1, d)),
            _const_spec((d, d)),
        ],
        out_specs=(row(d), row(d)),
        compiler_params=_cparams("arbitrary", "arbitrary"),
        name="oproj",
    )(a, cv, ch, x, mod4, mod4, mod4, g2, wo_b)


def _router_kernel(h_ref, wr_ref, br_ref, idx_ref, wt_ref, rank_ref, cnt_ref, carry_ref, *, tb):
    @pl.when(pl.program_id(0) == 0)
    def _():
        carry_ref[...] = jnp.zeros_like(carry_ref)

    logits = _dot3(wr_ref[...], h_ref[...], dot=_dot_nt) + br_ref[...]
    e_iota = lax.broadcasted_iota(I32, (N_EXPERTS, tb), 0)
    vals, idxs = [], []
    work = logits
    for _ in range(TOP_K):
        m = jnp.max(work, axis=0, keepdims=True)
        ik = jnp.min(jnp.where(work == m, e_iota, N_EXPERTS), axis=0, keepdims=True)
        vals.append(m)
        idxs.append(ik)
        work = jnp.where(e_iota == ik, -jnp.inf, work)
    exps = [jnp.exp(v - vals[0]) for v in vals]
    denom = functools.reduce(jnp.add, exps)
    wt_ref[...] = jnp.concatenate([e / denom for e in exps], axis=0)
    idx_ref[...] = jnp.concatenate(idxs, axis=0)

    onehot = functools.reduce(jnp.add, [(e_iota == ik).astype(F32) for ik in idxs])
    upper = (lax.broadcasted_iota(I32, (tb, tb), 0) < lax.broadcasted_iota(I32, (tb, tb), 1)).astype(BF16)
    before = _dot(onehot.astype(BF16), upper) + carry_ref[...]
    ranks = [jnp.sum(jnp.where(e_iota == ik, before, 0.0), axis=0, keepdims=True) for ik in idxs]
    rank_ref[...] = jnp.concatenate(ranks, axis=0).astype(I32)
    carry_ref[...] = carry_ref[...] + jnp.sum(onehot, axis=1, keepdims=True)
    cnt_ref[...] = jnp.broadcast_to(carry_ref[...], cnt_ref.shape)


def _router(h2, wr_t, br, *, tb):
    n, d = h2.shape
    tok = lambda dt: jax.ShapeDtypeStruct((TOP_K, n), dt)
    tok_spec = pl.BlockSpec((TOP_K, tb), lambda i: (0, i))
    return pl.pallas_call(
        functools.partial(_router_kernel, tb=tb),
        out_shape=(tok(I32), tok(F32), tok(I32), jax.ShapeDtypeStruct((N_EXPERTS, LANES), F32)),
        grid=(n // tb,),
        in_specs=[
            pl.BlockSpec((tb, d), lambda i: (i, 0)),
            pl.BlockSpec((N_EXPERTS, d), lambda i: (0, 0)),
            pl.BlockSpec((N_EXPERTS, 1), lambda i: (0, 0)),
        ],
        out_specs=(tok_spec, tok_spec, tok_spec, pl.BlockSpec((N_EXPERTS, LANES), lambda i: (0, 0))),
        scratch_shapes=[pltpu.VMEM((N_EXPERTS, 1), F32)],
        compiler_params=_cparams("arbitrary"),
        name="router",
    )(h2, wr_t, br)


def _pos_kernel(start_ref, idx_ref, rank_ref, pos_ref):
    idx = idx_ref[...]
    pos = rank_ref[...]
    for e in range(N_EXPERTS):
        pos = pos + jnp.where(idx == e, start_ref[e], 0)
    pos_ref[...] = pos


def _positions(starts, idx, rank):
    return pl.pallas_call(
        _pos_kernel,
        out_shape=jax.ShapeDtypeStruct(idx.shape, I32),
        grid_spec=pltpu.PrefetchScalarGridSpec(
            num_scalar_prefetch=1, grid=(1,),
            in_specs=[pl.BlockSpec(idx.shape, lambda i, s: (0, 0)), pl.BlockSpec(idx.shape, lambda i, s: (0, 0))],
            out_specs=pl.BlockSpec(idx.shape, lambda i, s: (0, 0))),
        compiler_params=_cparams("arbitrary"),
        name="positions",
    )(starts, idx, rank)


def _row_copy(src, src_row, dst, dst_row, sem):
    return pltpu.make_async_copy(src.at[pl.ds(src_row, 1)], dst.at[pl.ds(dst_row, 1)], sem)


def _dispatch_kernel(zf_ref, pos_ref, h_ref, xs_ref, zbuf_ref, sem, zsem, *, tb, tm, nblk):
    @pl.when(pl.program_id(0) == 0)
    def _():
        zbuf_ref[...] = jnp.zeros_like(zbuf_ref)

        def zero_copy(i):
            return pltpu.make_async_copy(zbuf_ref, xs_ref.at[pl.ds(pl.multiple_of(i * tm, tm), tm)], zsem)

        def start(i, carry):
            @pl.when(zf_ref[i] != 0)
            def _():
                zero_copy(i).start()
            return carry

        def wait(i, carry):
            @pl.when(zf_ref[i] != 0)
            def _():
                zero_copy(i).wait()
            return carry

        lax.fori_loop(0, nblk, start, 0)
        lax.fori_loop(0, nblk, wait, 0)

    def issue(t8, carry):
        t0 = pl.multiple_of(t8 * SUBLANES, SUBLANES)
        for r in range(SUBLANES):
            for k in range(TOP_K):
                _row_copy(h_ref, t0 + r, xs_ref, pos_ref[0, k * tb + t0 + r], sem).start(priority=(r + k) % 2)
        return carry

    lax.fori_loop(0, tb // SUBLANES, issue, 0)
    pltpu.make_async_copy(xs_ref.at[pl.ds(0, TOP_K * tb)], xs_ref.at[pl.ds(0, TOP_K * tb)], sem).wait()


def _dispatch(zero_flag, pos_blocks, h2, *, tb, tm):
    n, d = h2.shape
    nblk = zero_flag.shape[0]
    return pl.pallas_call(
        functools.partial(_dispatch_kernel, tb=tb, tm=tm, nblk=nblk),
        out_shape=jax.ShapeDtypeStruct((nblk * tm, d), F32),
        grid_spec=pltpu.PrefetchScalarGridSpec(
            num_scalar_prefetch=1, grid=(n // tb,),
            in_specs=[
                pl.BlockSpec((None, 1, TOP_K * tb), lambda i, zf: (i, 0, 0), memory_space=pltpu.SMEM),
                pl.BlockSpec((tb, d), lambda i, zf: (i, 0)),
            ],
            out_specs=pl.BlockSpec(memory_space=pl.ANY),
            scratch_shapes=[pltpu.VMEM((tm, d), F32), pltpu.SemaphoreType.DMA(()), pltpu.SemaphoreType.DMA(())]),
        compiler_params=_cparams("arbitrary"),
        name="dispatch",
    )(zero_flag, pos_blocks, h2)


def _expert_kernel(be_ref, ne_ref, nb_ref, xs_ref, wg_ref, bg_ref, wu_ref, bu_ref, wd_ref, bd_ref, ys_ref,
                   wf_ref, wb_ref, sems, *, layer):
    i = pl.program_id(0)
    used = i < nb_ref[0]
    new_expert = jnp.logical_or(i == 0, be_ref[i] != be_ref[jnp.maximum(i - 1, 0)])

    def fetch(e):
        return [pltpu.make_async_copy(w.at[layer, e], wf_ref.at[m], sems.at[m])
                for m, w in enumerate((wg_ref, wu_ref, wd_ref))]

    @pl.when(i == 0)
    def _():
        for cp in fetch(be_ref[0]):
            cp.start()

    @pl.when(jnp.logical_and(used, new_expert))
    def _():
        for m, cp in enumerate(fetch(be_ref[i])):
            cp.wait()
            wb_ref[m] = wf_ref[m].astype(BF16)
        nxt = ne_ref[i]

        @pl.when(nxt >= 0)
        def _():
            for cp in fetch(nxt):
                cp.start()

    @pl.when(used)
    def _():
        x = xs_ref[...].astype(BF16)
        g = jnp.minimum(_dot(x, wb_ref[0]) + bg_ref[...], SWIGLU_LIMIT)
        u = jnp.clip(_dot(x, wb_ref[1]) + bu_ref[...], -SWIGLU_LIMIT, SWIGLU_LIMIT)
        a = (u + 1.0) * (g * jax.nn.sigmoid(SWIGLU_ALPHA * g))
        ys_ref[...] = _dot(a.astype(BF16), wb_ref[2]) + bd_ref[...]

    @pl.when(jnp.logical_not(used))
    def _():
        ys_ref[...] = jnp.zeros_like(ys_ref)


def _experts(block_e, next_e, n_used, xs, layer, wg, bg, wu, bu, wd, bd, *, tm):
    p, d = xs.shape
    assert wg.shape[2:] == (d, d) and wd.shape[2:] == (d, d)
    bias = pl.BlockSpec((None, None, 1, d), lambda i, be, ne, nb: (layer, be[i], 0, 0))
    hbm = pl.BlockSpec(memory_space=pl.ANY)
    return pl.pallas_call(
        functools.partial(_expert_kernel, layer=layer),
        out_shape=jax.ShapeDtypeStruct((p, d), F32),
        grid_spec=pltpu.PrefetchScalarGridSpec(
            num_scalar_prefetch=3, grid=(p // tm,),
            in_specs=[pl.BlockSpec((tm, d), lambda i, be, ne, nb: (i, 0)), hbm, bias, hbm, bias, hbm, bias],
            out_specs=pl.BlockSpec((tm, d), lambda i, be, ne, nb: (i, 0)),
            scratch_shapes=[pltpu.VMEM((3, d, d), F32), pltpu.VMEM((3, d, d), BF16), pltpu.SemaphoreType.DMA((3,))]),
        compiler_params=_cparams("arbitrary"),
        name="experts",
    )(block_e, next_e, n_used, xs, wg, bg, wu, bu, wd, bd)


def _combine_kernel(pos_ref, posn_ref, ys_ref, wt_ref, x_ref, gt_ref, gf_ref, o_ref, buf_ref, sems, *, tb, final_norm):
    i = pl.program_id(0)
    n = pl.num_programs(0)

    def gather(p_ref, slot):
        def issue(t8, carry):
            t0 = pl.multiple_of(t8 * SUBLANES, SUBLANES)
            for r in range(SUBLANES):
                for k in range(TOP_K):
                    _row_copy(ys_ref, p_ref[0, k * tb + t0 + r], buf_ref.at[slot, k], t0 + r,
                              sems.at[slot]).start(priority=(r + k) % 2)
            return carry

        lax.fori_loop(0, tb // SUBLANES, issue, 0)

    @pl.when(i == 0)
    def _():
        gather(pos_ref, 0)

    @pl.when(i + 1 < n)
    def _():
        gather(posn_ref, (i + 1) % 2)

    slot = i % 2
    for k in range(TOP_K):
        pltpu.make_async_copy(ys_ref.at[pl.ds(0, tb)], buf_ref.at[slot, k], sems.at[slot]).wait()
    m = functools.reduce(jnp.add, [buf_ref[slot, k] * wt_ref[:, k:k + 1] for k in range(TOP_K)])
    xn = x_ref[...] + gt_ref[...] * m
    o_ref[...] = _rmsnorm(xn, gf_ref[...]) if final_norm else xn


def _combine(pos_blocks, ys, wt_tok, x, mod4, layer, row_of, g_final, blk0, *, tb, final_norm):
    bsz, s, d = x.shape
    nj = s // tb
    steps = bsz * nj
    return pl.pallas_call(
        functools.partial(_combine_kernel, tb=tb, final_norm=final_norm),
        out_shape=jax.ShapeDtypeStruct((bsz, s, d), F32),
        grid=(steps,),
        in_specs=[
            pl.BlockSpec((None, 1, TOP_K * tb), lambda i: (blk0 + i, 0, 0), memory_space=pltpu.SMEM),
            pl.BlockSpec((None, 1, TOP_K * tb), lambda i: (blk0 + jnp.minimum(i + 1, steps - 1), 0, 0),
                         memory_space=pltpu.SMEM),
            pl.BlockSpec(memory_space=pl.ANY),
            pl.BlockSpec((tb, TOP_K), lambda i: (blk0 + i, 0)),
            pl.BlockSpec((None, tb, d), lambda i: (i // nj, i % nj, 0)),
            pl.BlockSpec((None, None, 1, d), lambda i: (layer, row_of(i // nj), 0, 5)),
            pl.BlockSpec((1, d), lambda i: (0, 0)),
        ],
        out_specs=pl.BlockSpec((None, tb, d), lambda i: (i // nj, i % nj, 0)),
        scratch_shapes=[pltpu.VMEM((2, TOP_K, tb, d), F32), pltpu.SemaphoreType.DMA((2,))],
        compiler_params=_cparams("arbitrary"),
        name="combine",
    )(pos_blocks, pos_blocks, ys, wt_tok, x, mod4, g_final)


def _block_layout(pos, tb):
    n = pos.shape[1]
    return pos.reshape(TOP_K, n // tb, tb).transpose(1, 0, 2).reshape(n // tb, 1, TOP_K * tb)


def _moe(h2, wr_t, br, layer, wg, bg, wu, bu, wd, bd):
    n = h2.shape[0]
    tm = EXPERT_TILE
    td = _fit(n, DISPATCH_TILE)
    idx, wt, rank, cnt = _router(h2, wr_t, br, tb=_fit(n, ROUTER_TILE))
    counts = cnt[:, 0].astype(I32)
    padded = (counts + tm - 1) // tm * tm
    ends = jnp.cumsum(padded)
    pos = _positions((ends - padded).astype(I32), idx, rank)
    nblk = (n * TOP_K) // tm + N_EXPERTS
    n_used = ends[-1] // tm
    blk = jnp.arange(nblk, dtype=I32)
    first_row = jnp.minimum(blk, n_used - 1) * tm
    block_e = jnp.minimum(jnp.sum(ends[None, :] <= first_row[:, None], axis=1), N_EXPERTS - 1).astype(I32)
    partial_or_unused = (blk >= n_used) | jnp.any(((blk + 1) * tm)[:, None] == ends[None, :], axis=1)
    xs = _dispatch(partial_or_unused.astype(I32), _block_layout(pos, td), h2, tb=td, tm=tm)
    e_ids = jnp.arange(N_EXPERTS, dtype=I32)
    later = (e_ids[None, :] > e_ids[:, None]) & (padded[None, :] > 0)
    next_nonempty = jnp.min(jnp.where(later, e_ids[None, :], N_EXPERTS), axis=1)
    next_e = jnp.where(next_nonempty < N_EXPERTS, next_nonempty, -1).astype(I32)[block_e]
    ys = _experts(block_e, next_e, n_used.reshape(1).astype(I32), xs, layer, wg, bg, wu, bu, wd, bd, tm=tm)
    return ys, pos, wt


def _rope_tables(s):
    pos = jnp.arange(s)
    r, col = (pos // GRID_W).astype(F32), (pos % GRID_W).astype(F32)
    inv = ROPE_THETA ** (-jnp.arange(0, AXIS_DIM, 2, dtype=F32) / AXIS_DIM)
    ang = jnp.concatenate([r[:, None] * inv, col[:, None] * inv], axis=-1)
    cos, sin = jnp.cos(ang), jnp.sin(ang)
    return jnp.tile(jnp.concatenate([cos, cos], axis=-1), (1, 2)), jnp.tile(jnp.concatenate([-sin, sin], axis=-1), (1, 2))


def kernel(x, c, ctx, c_ctx, w_ada, b_ada, g_norm1, w_in, g_q, g_k, w_dw, b_dw, g_conv_ln, b_conv_ln,
           g_sgu_ln, b_sgu_ln, w_s, b_s, w_o, g_norm2, w_router, b_router, w_gate, b_gate, w_up, b_up,
           w_down, b_down, g_final):
    bsz, s, d = x.shape
    cl = ctx.shape[1]
    depth = w_ada.shape[0]
    n_lat, n_ctx = bsz * s, bsz * cl
    ctx_row = 8 * ((bsz + 7) // 8)
    rows = ctx_row + 8

    cs = jnp.zeros((rows, d), F32).at[:bsz].set(c).at[ctx_row].set(c_ctx)
    mod4 = _ada(cs, w_ada, b_ada).reshape(depth, rows, 1, 6 * d)
    lat_row = lambda b: b
    ctx_row_of = lambda b: ctx_row

    cos2, sin2 = _rope_tables(s)
    cosc = jnp.zeros((cl, LANES), F32)
    head_of = jnp.arange(ATTN_W) // HEAD_DIM
    bd = (head_of[:, None] == head_of[None, :]).astype(BF16)

    tm_lat = min(ROW_TILE, s)
    tm_ctx = min(ROW_TILE, cl)
    xc = ctx
    out = None
    for l in range(depth):
        last = l == depth - 1
        g1 = g_norm1[l].reshape(1, d)
        w_in_b = w_in[l].astype(BF16)
        gq = jnp.tile(g_q[l], N_Q_HEADS).reshape(1, ATTN_W)
        gk = jnp.tile(g_k[l], N_KV_HEADS).reshape(1, KV_W)
        conv_args = (w_dw[l], b_dw[l].reshape(1, -1), g_conv_ln[l].reshape(1, -1), b_conv_ln[l].reshape(1, -1))
        sgu_args = (g_sgu_ln[l].reshape(1, -1), b_sgu_ln[l].reshape(1, -1), w_s[l].astype(BF16),
                    jnp.repeat(b_s[l].T, CHUNK_W // CHUNK_HEADS, axis=1))
        wo_b = w_o[l].astype(BF16)
        g2 = g_norm2[l].reshape(1, d)

        q, kt, v, zc, zs = _proj(x, mod4, l, lat_row, g1, w_in_b, gq, gk, cos2, sin2, bd, rope=True, tm=tm_lat)
        if last:
            ktc, vc = _proj_kv(xc, mod4, l, ctx_row_of, g1, w_in_b[:, K0:CV0], gk, bd, tm=tm_ctx)
        else:
            qc, ktc, vc, zcc, zsc = _proj(xc, mod4, l, ctx_row_of, g1, w_in_b, gq, gk, cosc, cosc, bd,
                                          rope=False, tm=tm_ctx)
        a = _attention(q, [kt, ktc], [v, vc], tq=min(ATTN_Q_TILE, s))
        cv = _conv(zc, *conv_args, ts=min(CONV_TILE, s))
        ch = _sgu(zs, *sgu_args, tm=tm_lat)
        x, h2 = _oproj(a, cv, ch, x, mod4, l, lat_row, g2, wo_b, tm=tm_lat)
        tokens = h2.reshape(n_lat, d)
        if not last:
            ac = _attention(qc, [ktc], [vc], tq=min(ATTN_Q_TILE, cl))
            cvc = _conv(zcc, *conv_args, ts=min(CONV_TILE, cl))
            chc = _sgu(zsc, *sgu_args, tm=tm_ctx)
            xc, h2c = _oproj(ac, cvc, chc, xc, mod4, l, ctx_row_of, g2, wo_b, tm=tm_ctx)
            tokens = jnp.concatenate([tokens, h2c.reshape(n_ctx, d)], axis=0)

        ys, pos, wt = _moe(tokens, w_router[l].T, b_router[l].reshape(N_EXPERTS, 1), l,
                           w_gate, b_gate.reshape(depth, N_EXPERTS, 1, -1),
                           w_up, b_up.reshape(depth, N_EXPERTS, 1, -1),
                           w_down, b_down.reshape(depth, N_EXPERTS, 1, -1))
        tb = min(COMBINE_TILE, cl)
        pos_blocks = _block_layout(pos, tb)
        wt_tok = wt.T
        gf = g_final.reshape(1, d)
        x = _combine(pos_blocks, ys, wt_tok, x, mod4, l, lat_row, gf, 0, tb=tb, final_norm=last)
        if not last:
            xc = _combine(pos_blocks, ys, wt_tok, xc, mod4, l, ctx_row_of, gf, n_lat // tb, tb=tb, final_norm=False)
        out = x
    return out
```

```python
import functools

import jax
import jax.numpy as jnp
from jax import lax
from jax.experimental import pallas as pl
from jax.experimental.pallas import tpu as pltpu

F32 = jnp.float32
BF16 = jnp.bfloat16
I32 = jnp.int32

D_MODEL = 1024
GRID_W = 64
EPS = 1e-6
HEAD_DIM = 64
ATTN_W = 512
N_Q_HEADS = 8
N_KV_HEADS = 2
Q_GROUP = 4
KV_W = 128
ROPE_THETA = 10000.0
AXIS_DIM = 32
CONV_W = 256
CONV_K = 31
CONV_HALO = 16
CHUNK_W = 256
CHUNK_HEADS = 4
CHUNK = 128
Q0, K0, V0, CV0, CH0, IN_W = 0, 512, 640, 768, 1280, 1792
N_EXPERTS = 32
TOP_K = 4
SWIGLU_LIMIT = 7.0
SWIGLU_ALPHA = 1.702
LOG2_E = 1.4426950408889634

LANES = 128
SUBLANES = 8
MXU_COLS = 256
VMEM_LIMIT = 56 * 1024 * 1024

ROW_TILE = 512
ATTN_Q_TILE = 256
ATTN_K_TILE = 512
CONV_TILE = 512
CONV_ROWS = 64
ROUTER_TILE = 512
DISPATCH_TILE = 1024
EXPERT_TILE = 512
COMBINE_TILE = 256


def _fit(n, pref):
    return max(t for t in range(SUBLANES, min(n, pref) + 1, SUBLANES) if n % t == 0)


def _cparams(*sem):
    return pltpu.CompilerParams(dimension_semantics=sem, vmem_limit_bytes=VMEM_LIMIT)


def _split_bf16(a):
    hi = a.astype(BF16)
    lo = (a - hi.astype(F32)).astype(BF16)
    return hi, lo


def _dot(a, b):
    return jnp.dot(a, b, preferred_element_type=F32)


def _dot_nt(a, b):
    return lax.dot_general(a, b, (((1,), (1,)), ((), ())), preferred_element_type=F32)


def _dot3(a, b, dot=_dot):
    ah, al = _split_bf16(a)
    bh, bl = _split_bf16(b)
    return dot(ah, bh) + (dot(ah, bl) + dot(al, bh))


def _rmsnorm(x, g):
    return x * lax.rsqrt(jnp.mean(x * x, axis=-1, keepdims=True) + EPS) * g


def _layernorm(x, g, b):
    mu = jnp.mean(x, axis=-1, keepdims=True)
    xc = x - mu
    return xc * lax.rsqrt(jnp.mean(xc * xc, axis=-1, keepdims=True) + EPS) * g + b


def _ada_kernel(c_ref, w_ref, b_ref, o_ref):
    c = c_ref[...]
    s = c * jax.nn.sigmoid(c)
    o_ref[...] = _dot3(s, w_ref[...]) + b_ref[...]


def _ada(cs, w_ada, b_ada):
    depth, d, n6 = w_ada.shape
    r = cs.shape[0]
    tn = 1536
    return pl.pallas_call(
        _ada_kernel,
        out_shape=jax.ShapeDtypeStruct((depth, r, n6), F32),
        grid=(depth, n6 // tn),
        in_specs=[
            pl.BlockSpec((r, d), lambda l, j: (0, 0)),
            pl.BlockSpec((None, d, tn), lambda l, j: (l, 0, j)),
            pl.BlockSpec((None, 1, tn), lambda l, j: (l, 0, j)),
        ],
        out_specs=pl.BlockSpec((None, r, tn), lambda l, j: (l, 0, j)),
        compiler_params=_cparams("arbitrary", "arbitrary"),
        name="ada",
    )(cs, w_ada, b_ada.reshape(depth, 1, n6))


def _head_sumsq(t, bd):
    hi, lo = _split_bf16(t * t)
    return _dot(hi, bd) + _dot(lo, bd)


def _rope(t, cos, sin):
    w = t.shape[1]
    reps = w // LANES
    cosw = jnp.concatenate([cos] * reps, axis=1) if reps > 1 else cos
    sinw = jnp.concatenate([sin] * reps, axis=1) if reps > 1 else sin
    lane = lax.broadcasted_iota(I32, t.shape, 1)
    first = (lane % HEAD_DIM) < (HEAD_DIM // 2)
    rot = jnp.where(first, pltpu.roll(t, w - HEAD_DIM // 2, 1), pltpu.roll(t, HEAD_DIM // 2, 1))
    return t * cosw + rot * sinw


def _modulated(x, sh_ref, sc_ref, g_ref):
    return _rmsnorm(x, g_ref[...]) * (1.0 + sc_ref[...]) + sh_ref[...]


def _qk_head(t, g, bd, cos_ref, sin_ref, rope):
    t = t * lax.rsqrt(_head_sumsq(t, bd) * (1.0 / HEAD_DIM) + EPS) * g
    if rope:
        t = _rope(t, cos_ref[...], sin_ref[...])
    return t


PROJ_PHASES = 5


def _project(x, sh_ref, sc_ref, g1_ref, w_ref, gq_ref, gk_ref, cos_ref, sin_ref, bd_ref,
             q_ref, kt_ref, v_ref, zc_ref, zs_ref, rope, between=None):
    hook = between if between is not None else (lambda phase: None)
    hook(0)
    h = _modulated(x, sh_ref, sc_ref, g1_ref).astype(BF16)
    hook(1)
    q = _qk_head(_dot(h, w_ref[:, Q0:K0]), gq_ref[...], bd_ref[...], cos_ref, sin_ref, rope)
    q_ref[...] = (q * (HEAD_DIM ** -0.5 * LOG2_E)).astype(BF16)
    hook(2)
    kv = _dot(h, w_ref[:, K0:CV0])
    k = _qk_head(kv[:, 0:KV_W], gk_ref[...], bd_ref[0:KV_W, 0:KV_W], cos_ref, sin_ref, rope)
    kt_ref[...] = k.T.astype(BF16)
    v_ref[...] = kv[:, KV_W:].astype(BF16)
    hook(3)
    zc_ref[...] = _dot(h, w_ref[:, CV0:CH0])
    hook(4)
    zs_ref[...] = _dot(h, w_ref[:, CH0:IN_W])


def _proj_kernel(x_ref, *refs, rope):
    _project(x_ref[...], *refs, rope)


def _proj_kv_kernel(x_ref, sh_ref, sc_ref, g1_ref, w_ref, gk_ref, bd_ref, kt_ref, v_ref):
    h = _modulated(x_ref[...], sh_ref, sc_ref, g1_ref)
    p = _dot(h.astype(BF16), w_ref[...])
    k = _qk_head(p[:, 0:KV_W], gk_ref[...], bd_ref[0:KV_W, 0:KV_W], None, None, False)
    kt_ref[...] = k.T.astype(BF16)
    v_ref[...] = p[:, KV_W:2 * KV_W].astype(BF16)


def _mod_spec(layer, row_of, chunk):
    return pl.BlockSpec((None, None, 1, D_MODEL), lambda b, j: (layer, row_of(b), 0, chunk))


def _const_spec(shape):
    return pl.BlockSpec(shape, lambda b, j: tuple(0 for _ in shape))


def _proj(x, mod4, layer, row_of, g1, w_in_b, gq, gk, cos2, sin2, bd, *, rope, tm):
    bsz, s, d = x.shape
    grid = (bsz, s // tm)
    row = lambda w: pl.BlockSpec((None, tm, w), lambda b, j: (b, j, 0))
    return pl.pallas_call(
        functools.partial(_proj_kernel, rope=rope),
        out_shape=(
            jax.ShapeDtypeStruct((bsz, s, ATTN_W), BF16),
            jax.ShapeDtypeStruct((bsz, KV_W, s), BF16),
            jax.ShapeDtypeStruct((bsz, s, KV_W), BF16),
            jax.ShapeDtypeStruct((bsz, s, 2 * CONV_W), F32),
            jax.ShapeDtypeStruct((bsz, s, 2 * CHUNK_W), F32),
        ),
        grid=grid,
        in_specs=[
            row(d),
            _mod_spec(layer, row_of, 0),
            _mod_spec(layer, row_of, 1),
            _const_spec((1, d)),
            _const_spec((d, IN_W)),
            _const_spec((1, ATTN_W)),
            _const_spec((1, KV_W)),
            pl.BlockSpec((tm, LANES), lambda b, j: (j, 0)),
            pl.BlockSpec((tm, LANES), lambda b, j: (j, 0)),
            _const_spec((ATTN_W, ATTN_W)),
        ],
        out_specs=(
            row(ATTN_W),
            pl.BlockSpec((None, KV_W, tm), lambda b, j: (b, 0, j)),
            row(KV_W),
            row(2 * CONV_W),
            row(2 * CHUNK_W),
        ),
        compiler_params=_cparams("arbitrary", "arbitrary"),
        name="proj",
    )(x, mod4, mod4, g1, w_in_b, gq, gk, cos2, sin2, bd)


def _proj_kv(x, mod4, layer, row_of, g1, w_kv_b, gk, bd, *, tm):
    bsz, s, d = x.shape
    return pl.pallas_call(
        _proj_kv_kernel,
        out_shape=(
            jax.ShapeDtypeStruct((bsz, KV_W, s), BF16),
            jax.ShapeDtypeStruct((bsz, s, KV_W), BF16),
        ),
        grid=(bsz, s // tm),
        in_specs=[
            pl.BlockSpec((None, tm, d), lambda b, j: (b, j, 0)),
            _mod_spec(layer, row_of, 0),
            _mod_spec(layer, row_of, 1),
            _const_spec((1, d)),
            _const_spec((d, 2 * KV_W)),
            _const_spec((1, KV_W)),
            _const_spec((ATTN_W, ATTN_W)),
        ],
        out_specs=(
            pl.BlockSpec((None, KV_W, tm), lambda b, j: (b, 0, j)),
            pl.BlockSpec((None, tm, KV_W), lambda b, j: (b, j, 0)),
        ),
        compiler_params=_cparams("arbitrary", "arbitrary"),
        name="proj_kv",
    )(x, mod4, mod4, g1, w_kv_b, gk, bd)


def _attn_kernel(*refs, n_src):
    q_ref = refs[0]
    kt_refs = refs[1:1 + n_src]
    v_refs = refs[1 + n_src:1 + 2 * n_src]
    o_ref, s_ref, p_ref = refs[1 + 2 * n_src:]
    tiles, spans, col = [], [], 0
    for i, kt in enumerate(kt_refs):
        kn = kt.shape[1]
        tk = next(w for w in (ATTN_K_TILE, MXU_COLS, LANES) if kn % w == 0)
        tiles += [(i, o, col + o, tk) for o in range(0, kn, tk)]
        spans.append((col, kn))
        col += kn

    def scores_into(slot, h):
        g = h // Q_GROUP
        qh = q_ref[:, h * HEAD_DIM:(h + 1) * HEAD_DIM]
        mx = None
        for i, o, c, tk in tiles:
            tile = _dot(qh, kt_refs[i][g * HEAD_DIM:(g + 1) * HEAD_DIM, o:o + tk])
            s_ref[slot, :, c:c + tk] = tile
            for lo in range(0, tk, LANES):
                mx = tile[:, lo:lo + LANES] if mx is None else jnp.maximum(mx, tile[:, lo:lo + LANES])
        return jnp.max(mx, axis=-1, keepdims=True)

    m_next = scores_into(0, 0)
    outs = []
    for h in range(N_Q_HEADS):
        g, slot, m = h // Q_GROUP, h % 2, m_next
        if h + 1 < N_Q_HEADS:
            m_next = scores_into(1 - slot, h + 1)
        p = jnp.exp2(s_ref[slot] - m)
        denom = jnp.sum(p, axis=-1, keepdims=True)
        p_ref[slot] = p.astype(BF16)
        o = functools.reduce(jnp.add, [_dot(p_ref[slot, :, c:c + kn], v[...]) for (c, kn), v in zip(spans, v_refs)])
        outs.append(o[:, g * HEAD_DIM:(g + 1) * HEAD_DIM] / denom)
    o_ref[...] = jnp.concatenate(outs, axis=1).astype(BF16)


def _attention(q, kts, vs, *, tq):
    bsz, nq, _ = q.shape
    n_src = len(kts)
    kn_all = sum(kt.shape[2] for kt in kts)
    in_specs = [pl.BlockSpec((None, tq, ATTN_W), lambda b, j: (b, j, 0))]
    in_specs += [pl.BlockSpec((None, KV_W, kt.shape[2]), lambda b, j: (b, 0, 0)) for kt in kts]
    in_specs += [pl.BlockSpec((None, v.shape[1], KV_W), lambda b, j: (b, 0, 0)) for v in vs]
    return pl.pallas_call(
        functools.partial(_attn_kernel, n_src=n_src),
        out_shape=jax.ShapeDtypeStruct((bsz, nq, ATTN_W), BF16),
        grid=(bsz, nq // tq),
        in_specs=in_specs,
        out_specs=pl.BlockSpec((None, tq, ATTN_W), lambda b, j: (b, j, 0)),
        scratch_shapes=[pltpu.VMEM((2, tq, kn_all), F32), pltpu.VMEM((2, tq, kn_all), BF16)],
        compiler_params=_cparams("arbitrary", "arbitrary"),
        name="attention",
    )(q, *kts, *vs)


def _glu(z):
    return z[:, :CONV_W] * jax.nn.sigmoid(z[:, CONV_W:])


def _conv_kernel(zp_ref, z_ref, zn_ref, w_ref, bdw_ref, g_ref, b_ref, o_ref, u_ref, us_ref, *, ts):
    j = pl.program_id(1)
    nj = pl.num_programs(1)
    u_ref[0:CONV_HALO, :] = jnp.where(j > 0, _glu(zp_ref[...]), 0.0)
    u_ref[CONV_HALO + ts:CONV_HALO + ts + CONV_HALO, :] = jnp.where(j < nj - 1, _glu(zn_ref[...]), 0.0)
    u_ref[CONV_HALO:CONV_HALO + ts, :] = _glu(z_ref[...])
    rows = us_ref.shape[1]
    for b in range(1, SUBLANES):
        us_ref[b - 1] = u_ref[b:b + rows, :]
    off = CONV_HALO - CONV_K // 2
    for c in range(ts // CONV_ROWS):
        acc = jnp.zeros((CONV_ROWS, CONV_W), F32)
        for k in range(CONV_K):
            start = c * CONV_ROWS + off + k
            b, lo = start % SUBLANES, start - start % SUBLANES
            window = u_ref[lo:lo + CONV_ROWS, :] if b == 0 else us_ref[b - 1, lo:lo + CONV_ROWS, :]
            acc = acc + window * w_ref[k:k + 1, :]
        y = _layernorm(acc + bdw_ref[...], g_ref[...], b_ref[...])
        o_ref[c * CONV_ROWS:(c + 1) * CONV_ROWS, :] = (y * jax.nn.sigmoid(y)).astype(BF16)


def _conv(zc, w_dw, b_dw, g_ln, b_ln, *, ts):
    bsz, s, _ = zc.shape
    nh = ts // CONV_HALO
    last = s // CONV_HALO - 1
    return pl.pallas_call(
        functools.partial(_conv_kernel, ts=ts),
        out_shape=jax.ShapeDtypeStruct((bsz, s, CONV_W), BF16),
        grid=(bsz, s // ts),
        in_specs=[
            pl.BlockSpec((None, CONV_HALO, 2 * CONV_W), lambda b, j: (b, jnp.maximum(j * nh - 1, 0), 0)),
            pl.BlockSpec((None, ts, 2 * CONV_W), lambda b, j: (b, j, 0)),
            pl.BlockSpec((None, CONV_HALO, 2 * CONV_W), lambda b, j: (b, jnp.minimum((j + 1) * nh, last), 0)),
            _const_spec((CONV_K, CONV_W)),
            _const_spec((1, CONV_W)),
            _const_spec((1, CONV_W)),
            _const_spec((1, CONV_W)),
        ],
        out_specs=pl.BlockSpec((None, ts, CONV_W), lambda b, j: (b, j, 0)),
        scratch_shapes=[pltpu.VMEM((ts + 2 * CONV_HALO, CONV_W), F32),
                        pltpu.VMEM((SUBLANES - 1, ts + 2 * CONV_HALO - SUBLANES, CONV_W), F32)],
        compiler_params=_cparams("arbitrary", "arbitrary"),
        name="conv",
    )(zc, zc, zc, w_dw, b_dw, g_ln, b_ln)


def _sgu_kernel(z_ref, g_ref, b_ref, ws_ref, bs_ref, o_ref, *, tm):
    z = z_ref[...]
    z = 0.5 * z * (1.0 + lax.erf(z * (2.0 ** -0.5)))
    u = z[:, :CHUNK_W]
    v = _layernorm(z[:, CHUNK_W:], g_ref[...], b_ref[...]).astype(BF16)
    head = lax.broadcasted_iota(I32, (CHUNK, CHUNK_W), 1) // (CHUNK_W // CHUNK_HEADS)
    for c in range(tm // CHUNK):
        vc = v[c * CHUNK:(c + 1) * CHUNK, :]
        s = bs_ref[...]
        for h in range(CHUNK_HEADS):
            s = s + jnp.where(head == h, _dot(ws_ref[h], vc), 0.0)
        o_ref[c * CHUNK:(c + 1) * CHUNK, :] = (u[c * CHUNK:(c + 1) * CHUNK, :] * s).astype(BF16)


def _sgu(zs, g_ln, b_ln, ws_b, bs_full, *, tm):
    bsz, s, _ = zs.shape
    return pl.pallas_call(
        functools.partial(_sgu_kernel, tm=tm),
        out_shape=jax.ShapeDtypeStruct((bsz, s, CHUNK_W), BF16),
        grid=(bsz, s // tm),
        in_specs=[
            pl.BlockSpec((None, tm, 2 * CHUNK_W), lambda b, j: (b, j, 0)),
            _const_spec((1, CHUNK_W)),
            _const_spec((1, CHUNK_W)),
            _const_spec((CHUNK_HEADS, CHUNK, CHUNK)),
            _const_spec((CHUNK, CHUNK_W)),
        ],
        out_specs=pl.BlockSpec((None, tm, CHUNK_W), lambda b, j: (b, j, 0)),
        compiler_params=_cparams("arbitrary", "arbitrary"),
        name="sgu",
    )(zs, g_ln, b_ln, ws_b, bs_full)


def _oproj_kernel(a_ref, cv_ref, ch_ref, x_ref, gt_ref, sh_ref, sc_ref, g2_ref, wo_ref, xo_ref, h2_ref):
    y = (_dot(a_ref[...], wo_ref[0:ATTN_W, :])
         + _dot(cv_ref[...], wo_ref[ATTN_W:ATTN_W + CONV_W, :])
         + _dot(ch_ref[...], wo_ref[ATTN_W + CONV_W:, :]))
    xn = x_ref[...] + gt_ref[...] * y
    xo_ref[...] = xn
    h2_ref[...] = _rmsnorm(xn, g2_ref[...]) * (1.0 + sc_ref[...]) + sh_ref[...]


def _oproj(a, cv, ch, x, mod4, layer, row_of, g2, wo_b, *, tm):
    bsz, s, d = x.shape
    row = lambda w: pl.BlockSpec((None, tm, w), lambda b, j: (b, j, 0))
    return pl.pallas_call(
        _oproj_kernel,
        out_shape=(jax.ShapeDtypeStruct((bsz, s, d), F32), jax.ShapeDtypeStruct((bsz, s, d), F32)),
        grid=(bsz, s // tm),
        in_specs=[
            row(ATTN_W), row(CONV_W), row(CHUNK_W), row(d),
            _mod_spec(layer, row_of, 2),
            _mod_spec(layer, row_of, 3),
            _mod_spec(layer, row_of, 4),
            _const_spec((1, d)),
            _const_spec((d, d)),
        ],
        out_specs=(row(d), row(d)),
        compiler_params=_cparams("arbitrary", "arbitrary"),
        name="oproj",
    )(a, cv, ch, x, mod4, mod4, mod4, g2, wo_b)


def _router_kernel(h_ref, wr_ref, br_ref, idx_ref, wt_ref, rank_ref, cnt_ref, carry_ref, *, tb):
    @pl.when(pl.program_id(0) == 0)
    def _():
        carry_ref[...] = jnp.zeros_like(carry_ref)

    logits = _dot3(wr_ref[...], h_ref[...], dot=_dot_nt) + br_ref[...]
    e_iota = lax.broadcasted_iota(I32, (N_EXPERTS, tb), 0)
    vals, idxs = [], []
    work = logits
    for _ in range(TOP_K):
        m = jnp.max(work, axis=0, keepdims=True)
        ik = jnp.min(jnp.where(work == m, e_iota, N_EXPERTS), axis=0, keepdims=True)
        vals.append(m)
        idxs.append(ik)
        work = jnp.where(e_iota == ik, -jnp.inf, work)
    exps = [jnp.exp(v - vals[0]) for v in vals]
    denom = functools.reduce(jnp.add, exps)
    wt_ref[...] = jnp.concatenate([e / denom for e in exps], axis=0)
    idx_ref[...] = jnp.concatenate(idxs, axis=0)

    onehot = functools.reduce(jnp.add, [(e_iota == ik).astype(F32) for ik in idxs])
    upper = (lax.broadcasted_iota(I32, (tb, tb), 0) < lax.broadcasted_iota(I32, (tb, tb), 1)).astype(BF16)
    before = _dot(onehot.astype(BF16), upper) + carry_ref[...]
    ranks = [jnp.sum(jnp.where(e_iota == ik, before, 0.0), axis=0, keepdims=True) for ik in idxs]
    rank_ref[...] = jnp.concatenate(ranks, axis=0).astype(I32)
    carry_ref[...] = carry_ref[...] + jnp.sum(onehot, axis=1, keepdims=True)
    cnt_ref[...] = jnp.broadcast_to(carry_ref[...], cnt_ref.shape)


def _router(h2, wr_t, br, *, tb):
    n, d = h2.shape
    tok = lambda dt: jax.ShapeDtypeStruct((TOP_K, n), dt)
    tok_spec = pl.BlockSpec((TOP_K, tb), lambda i: (0, i))
    return pl.pallas_call(
        functools.partial(_router_kernel, tb=tb),
        out_shape=(tok(I32), tok(F32), tok(I32), jax.ShapeDtypeStruct((N_EXPERTS, LANES), F32)),
        grid=(n // tb,),
        in_specs=[
            pl.BlockSpec((tb, d), lambda i: (i, 0)),
            pl.BlockSpec((N_EXPERTS, d), lambda i: (0, 0)),
            pl.BlockSpec((N_EXPERTS, 1), lambda i: (0, 0)),
        ],
        out_specs=(tok_spec, tok_spec, tok_spec, pl.BlockSpec((N_EXPERTS, LANES), lambda i: (0, 0))),
        scratch_shapes=[pltpu.VMEM((N_EXPERTS, 1), F32)],
        compiler_params=_cparams("arbitrary"),
        name="router",
    )(h2, wr_t, br)


def _pos_kernel(start_ref, idx_ref, rank_ref, pos_ref):
    idx = idx_ref[...]
    pos = rank_ref[...]
    for e in range(N_EXPERTS):
        pos = pos + jnp.where(idx == e, start_ref[e], 0)
    pos_ref[...] = pos


def _positions(starts, idx, rank):
    return pl.pallas_call(
        _pos_kernel,
        out_shape=jax.ShapeDtypeStruct(idx.shape, I32),
        grid_spec=pltpu.PrefetchScalarGridSpec(
            num_scalar_prefetch=1, grid=(1,),
            in_specs=[pl.BlockSpec(idx.shape, lambda i, s: (0, 0)), pl.BlockSpec(idx.shape, lambda i, s: (0, 0))],
            out_specs=pl.BlockSpec(idx.shape, lambda i, s: (0, 0))),
        compiler_params=_cparams("arbitrary"),
        name="positions",
    )(starts, idx, rank)


def _row_copy(src, src_row, dst, dst_row, sem):
    return pltpu.make_async_copy(src.at[pl.ds(src_row, 1)], dst.at[pl.ds(dst_row, 1)], sem)


def _dispatch_kernel(zf_ref, pos_ref, h_ref, xs_ref, zbuf_ref, sem, zsem, *, tb, tm, nblk):
    @pl.when(pl.program_id(0) == 0)
    def _():
        zbuf_ref[...] = jnp.zeros_like(zbuf_ref)

        def zero_copy(i):
            return pltpu.make_async_copy(zbuf_ref, xs_ref.at[pl.ds(pl.multiple_of(i * tm, tm), tm)], zsem)

        def start(i, carry):
            @pl.when(zf_ref[i] != 0)
            def _():
                zero_copy(i).start()
            return carry

        def wait(i, carry):
            @pl.when(zf_ref[i] != 0)
            def _():
                zero_copy(i).wait()
            return carry

        lax.fori_loop(0, nblk, start, 0)
        lax.fori_loop(0, nblk, wait, 0)

    def issue(t8, carry):
        t0 = pl.multiple_of(t8 * SUBLANES, SUBLANES)
        for r in range(SUBLANES):
            for k in range(TOP_K):
                _row_copy(h_ref, t0 + r, xs_ref, pos_ref[0, k * tb + t0 + r], sem).start(priority=(r + k) % 2)
        return carry

    lax.fori_loop(0, tb // SUBLANES, issue, 0)
    pltpu.make_async_copy(xs_ref.at[pl.ds(0, TOP_K * tb)], xs_ref.at[pl.ds(0, TOP_K * tb)], sem).wait()


def _dispatch(zero_flag, pos_blocks, h2, *, tb, tm):
    n, d = h2.shape
    nblk = zero_flag.shape[0]
    return pl.pallas_call(
        functools.partial(_dispatch_kernel, tb=tb, tm=tm, nblk=nblk),
        out_shape=jax.ShapeDtypeStruct((nblk * tm, d), F32),
        grid_spec=pltpu.PrefetchScalarGridSpec(
            num_scalar_prefetch=1, grid=(n // tb,),
            in_specs=[
                pl.BlockSpec((None, 1, TOP_K * tb), lambda i, zf: (i, 0, 0), memory_space=pltpu.SMEM),
                pl.BlockSpec((tb, d), lambda i, zf: (i, 0)),
            ],
            out_specs=pl.BlockSpec(memory_space=pl.ANY),
            scratch_shapes=[pltpu.VMEM((tm, d), F32), pltpu.SemaphoreType.DMA(()), pltpu.SemaphoreType.DMA(())]),
        compiler_params=_cparams("arbitrary"),
        name="dispatch",
    )(zero_flag, pos_blocks, h2)


def _expert_kernel(be_ref, ne_ref, nb_ref, xs_ref, wg_ref, bg_ref, wu_ref, bu_ref, wd_ref, bd_ref, ys_ref,
                   wf_ref, wb_ref, sems, *, layer):
    i = pl.program_id(0)
    used = i < nb_ref[0]
    new_expert = jnp.logical_or(i == 0, be_ref[i] != be_ref[jnp.maximum(i - 1, 0)])

    def fetch(e):
        return [pltpu.make_async_copy(w.at[layer, e], wf_ref.at[m], sems.at[m])
                for m, w in enumerate((wg_ref, wu_ref, wd_ref))]

    @pl.when(i == 0)
    def _():
        for cp in fetch(be_ref[0]):
            cp.start()

    @pl.when(jnp.logical_and(used, new_expert))
    def _():
        for m, cp in enumerate(fetch(be_ref[i])):
            cp.wait()
            wb_ref[m] = wf_ref[m].astype(BF16)
        nxt = ne_ref[i]

        @pl.when(nxt >= 0)
        def _():
            for cp in fetch(nxt):
                cp.start()

    @pl.when(used)
    def _():
        x = xs_ref[...].astype(BF16)
        g = jnp.minimum(_dot(x, wb_ref[0]) + bg_ref[...], SWIGLU_LIMIT)
        u = jnp.clip(_dot(x, wb_ref[1]) + bu_ref[...], -SWIGLU_LIMIT, SWIGLU_LIMIT)
        a = (u + 1.0) * (g * jax.nn.sigmoid(SWIGLU_ALPHA * g))
        ys_ref[...] = _dot(a.astype(BF16), wb_ref[2]) + bd_ref[...]

    @pl.when(jnp.logical_not(used))
    def _():
        ys_ref[...] = jnp.zeros_like(ys_ref)


def _experts(block_e, next_e, n_used, xs, layer, wg, bg, wu, bu, wd, bd, *, tm):
    p, d = xs.shape
    assert wg.shape[2:] == (d, d) and wd.shape[2:] == (d, d)
    bias = pl.BlockSpec((None, None, 1, d), lambda i, be, ne, nb: (layer, be[i], 0, 0))
    hbm = pl.BlockSpec(memory_space=pl.ANY)
    return pl.pallas_call(
        functools.partial(_expert_kernel, layer=layer),
        out_shape=jax.ShapeDtypeStruct((p, d), F32),
        grid_spec=pltpu.PrefetchScalarGridSpec(
            num_scalar_prefetch=3, grid=(p // tm,),
            in_specs=[pl.BlockSpec((tm, d), lambda i, be, ne, nb: (i, 0)), hbm, bias, hbm, bias, hbm, bias],
            out_specs=pl.BlockSpec((tm, d), lambda i, be, ne, nb: (i, 0)),
            scratch_shapes=[pltpu.VMEM((3, d, d), F32), pltpu.VMEM((3, d, d), BF16), pltpu.SemaphoreType.DMA((3,))]),
        compiler_params=_cparams("arbitrary"),
        name="experts",
    )(block_e, next_e, n_used, xs, wg, bg, wu, bu, wd, bd)


def _combine_kernel(pos_ref, posn_ref, ys_ref, wt_ref, x_ref, gt_ref, gf_ref, o_ref, buf_ref, sems, *, tb, final_norm):
    i = pl.program_id(0)
    n = pl.num_programs(0)

    def gather(p_ref, slot):
        def issue(t8, carry):
            t0 = pl.multiple_of(t8 * SUBLANES, SUBLANES)
            for r in range(SUBLANES):
                for k in range(TOP_K):
                    _row_copy(ys_ref, p_ref[0, k * tb + t0 + r], buf_ref.at[slot, k], t0 + r,
                              sems.at[slot]).start(priority=(r + k) % 2)
            return carry

        lax.fori_loop(0, tb // SUBLANES, issue, 0)

    @pl.when(i == 0)
    def _():
        gather(pos_ref, 0)

    @pl.when(i + 1 < n)
    def _():
        gather(posn_ref, (i + 1) % 2)

    slot = i % 2
    for k in range(TOP_K):
        pltpu.make_async_copy(ys_ref.at[pl.ds(0, tb)], buf_ref.at[slot, k], sems.at[slot]).wait()
    m = functools.reduce(jnp.add, [buf_ref[slot, k] * wt_ref[:, k:k + 1] for k in range(TOP_K)])
    xn = x_ref[...] + gt_ref[...] * m
    o_ref[...] = _rmsnorm(xn, gf_ref[...]) if final_norm else xn


def _combine(pos_blocks, ys, wt_tok, x, mod4, layer, row_of, g_final, blk0, *, tb, final_norm):
    bsz, s, d = x.shape
    nj = s // tb
    steps = bsz * nj
    return pl.pallas_call(
        functools.partial(_combine_kernel, tb=tb, final_norm=final_norm),
        out_shape=jax.ShapeDtypeStruct((bsz, s, d), F32),
        grid=(steps,),
        in_specs=[
            pl.BlockSpec((None, 1, TOP_K * tb), lambda i: (blk0 + i, 0, 0), memory_space=pltpu.SMEM),
            pl.BlockSpec((None, 1, TOP_K * tb), lambda i: (blk0 + jnp.minimum(i + 1, steps - 1), 0, 0),
                         memory_space=pltpu.SMEM),
            pl.BlockSpec(memory_space=pl.ANY),
            pl.BlockSpec((tb, TOP_K), lambda i: (blk0 + i, 0)),
            pl.BlockSpec((None, tb, d), lambda i: (i // nj, i % nj, 0)),
            pl.BlockSpec((None, None, 1, d), lambda i: (layer, row_of(i // nj), 0, 5)),
            pl.BlockSpec((1, d), lambda i: (0, 0)),
        ],
        out_specs=pl.BlockSpec((None, tb, d), lambda i: (i // nj, i % nj, 0)),
        scratch_shapes=[pltpu.VMEM((2, TOP_K, tb, d), F32), pltpu.SemaphoreType.DMA((2,))],
        compiler_params=_cparams("arbitrary"),
        name="combine",
    )(pos_blocks, pos_blocks, ys, wt_tok, x, mod4, g_final)


def _combine_proj_kernel(pos_ref, posn_ref, ys_ref, wt_ref, x_ref, gt_ref, *refs, tb):
    proj_refs, xo_ref, out_refs, buf_ref, sems = refs[:9], refs[9], refs[10:15], refs[15], refs[16]
    i = pl.program_id(0)
    n = pl.num_programs(0)

    def drain(slot):
        for k in range(TOP_K):
            pltpu.make_async_copy(ys_ref.at[pl.ds(0, tb)], buf_ref.at[slot, k], sems.at[slot]).wait()

    @pl.when(i == 0)
    def _():
        def issue(t, carry):
            for k in range(TOP_K):
                _row_copy(ys_ref, pos_ref[0, k * tb + t], buf_ref.at[0, k], t, sems.at[0]).start(priority=k % 2)
            return carry

        lax.fori_loop(0, tb, issue, 0)

    slot = i % 2
    drain(slot)
    m = functools.reduce(jnp.add, [buf_ref[slot, k] * wt_ref[:, k:k + 1] for k in range(TOP_K)])
    xn = x_ref[...] + gt_ref[...] * m
    xo_ref[...] = xn

    def gather_slice(phase):
        for t in range(phase * tb // PROJ_PHASES, (phase + 1) * tb // PROJ_PHASES):
            for k in range(TOP_K):
                _row_copy(ys_ref, posn_ref[0, k * tb + t], buf_ref.at[1 - slot, k], t,
                          sems.at[1 - slot]).start(priority=(t + k) % 2)

    _project(xn, *proj_refs, *out_refs, True, between=gather_slice)

    @pl.when(i == n - 1)
    def _():
        drain(1 - slot)


def _combine_proj(pos_blocks, ys, wt_tok, x, mod4, prev_layer, layer, g1, w_in_b, gq, gk, cos2, sin2, bd, *, tm):
    bsz, s, d = x.shape
    nj = s // tm
    steps = bsz * nj
    row = lambda w: pl.BlockSpec((None, tm, w), lambda i: (i // nj, i % nj, 0))
    mod = lambda l, c: pl.BlockSpec((None, None, 1, d), lambda i: (l, i // nj, 0, c))
    const = lambda shape: pl.BlockSpec(shape, lambda i: tuple(0 for _ in shape))
    smem = lambda f: pl.BlockSpec((None, 1, TOP_K * tm), lambda i: (f(i), 0, 0), memory_space=pltpu.SMEM)
    return pl.pallas_call(
        functools.partial(_combine_proj_kernel, tb=tm),
        out_shape=(
            jax.ShapeDtypeStruct((bsz, s, d), F32),
            jax.ShapeDtypeStruct((bsz, s, ATTN_W), BF16),
            jax.ShapeDtypeStruct((bsz, KV_W, s), BF16),
            jax.ShapeDtypeStruct((bsz, s, KV_W), BF16),
            jax.ShapeDtypeStruct((bsz, s, 2 * CONV_W), F32),
            jax.ShapeDtypeStruct((bsz, s, 2 * CHUNK_W), F32),
        ),
        grid=(steps,),
        in_specs=[
            smem(lambda i: i),
            smem(lambda i: jnp.minimum(i + 1, steps - 1)),
            pl.BlockSpec(memory_space=pl.ANY),
            pl.BlockSpec((tm, TOP_K), lambda i: (i, 0)),
            row(d),
            mod(prev_layer, 5),
            mod(layer, 0),
            mod(layer, 1),
            const((1, d)),
            const((d, IN_W)),
            const((1, ATTN_W)),
            const((1, KV_W)),
            pl.BlockSpec((tm, LANES), lambda i: (i % nj, 0)),
            pl.BlockSpec((tm, LANES), lambda i: (i % nj, 0)),
            const((ATTN_W, ATTN_W)),
        ],
        out_specs=(
            row(d),
            row(ATTN_W),
            pl.BlockSpec((None, KV_W, tm), lambda i: (i // nj, 0, i % nj)),
            row(KV_W),
            row(2 * CONV_W),
            row(2 * CHUNK_W),
        ),
        scratch_shapes=[pltpu.VMEM((2, TOP_K, tm, d), F32), pltpu.SemaphoreType.DMA((2,))],
        compiler_params=_cparams("arbitrary"),
        name="combine_proj",
    )(pos_blocks, pos_blocks, ys, wt_tok, x, mod4, mod4, mod4, g1, w_in_b, gq, gk, cos2, sin2, bd)


def _block_layout(pos, tb):
    n = pos.shape[1]
    return pos.reshape(TOP_K, n // tb, tb).transpose(1, 0, 2).reshape(n // tb, 1, TOP_K * tb)


def _moe(h2, wr_t, br, layer, wg, bg, wu, bu, wd, bd):
    n = h2.shape[0]
    tm = EXPERT_TILE
    td = _fit(n, DISPATCH_TILE)
    idx, wt, rank, cnt = _router(h2, wr_t, br, tb=_fit(n, ROUTER_TILE))
    counts = cnt[:, 0].astype(I32)
    padded = (counts + tm - 1) // tm * tm
    ends = jnp.cumsum(padded)
    pos = _positions((ends - padded).astype(I32), idx, rank)
    nblk = (n * TOP_K) // tm + N_EXPERTS
    n_used = ends[-1] // tm
    blk = jnp.arange(nblk, dtype=I32)
    first_row = jnp.minimum(blk, n_used - 1) * tm
    block_e = jnp.minimum(jnp.sum(ends[None, :] <= first_row[:, None], axis=1), N_EXPERTS - 1).astype(I32)
    partial_or_unused = (blk >= n_used) | jnp.any(((blk + 1) * tm)[:, None] == ends[None, :], axis=1)
    xs = _dispatch(partial_or_unused.astype(I32), _block_layout(pos, td), h2, tb=td, tm=tm)
    e_ids = jnp.arange(N_EXPERTS, dtype=I32)
    later = (e_ids[None, :] > e_ids[:, None]) & (padded[None, :] > 0)
    next_nonempty = jnp.min(jnp.where(later, e_ids[None, :], N_EXPERTS), axis=1)
    next_e = jnp.where(next_nonempty < N_EXPERTS, next_nonempty, -1).astype(I32)[block_e]
    ys = _experts(block_e, next_e, n_used.reshape(1).astype(I32), xs, layer, wg, bg, wu, bu, wd, bd, tm=tm)
    return ys, pos, wt


def _rope_tables(s):
    pos = jnp.arange(s)
    r, col = (pos // GRID_W).astype(F32), (pos % GRID_W).astype(F32)
    inv = ROPE_THETA ** (-jnp.arange(0, AXIS_DIM, 2, dtype=F32) / AXIS_DIM)
    ang = jnp.concatenate([r[:, None] * inv, col[:, None] * inv], axis=-1)
    cos, sin = jnp.cos(ang), jnp.sin(ang)
    return jnp.tile(jnp.concatenate([cos, cos], axis=-1), (1, 2)), jnp.tile(jnp.concatenate([-sin, sin], axis=-1), (1, 2))


def kernel(x, c, ctx, c_ctx, w_ada, b_ada, g_norm1, w_in, g_q, g_k, w_dw, b_dw, g_conv_ln, b_conv_ln,
           g_sgu_ln, b_sgu_ln, w_s, b_s, w_o, g_norm2, w_router, b_router, w_gate, b_gate, w_up, b_up,
           w_down, b_down, g_final):
    bsz, s, d = x.shape
    cl = ctx.shape[1]
    depth = w_ada.shape[0]
    n_lat, n_ctx = bsz * s, bsz * cl
    ctx_row = 8 * ((bsz + 7) // 8)
    rows = ctx_row + 8

    cs = jnp.zeros((rows, d), F32).at[:bsz].set(c).at[ctx_row].set(c_ctx)
    mod4 = _ada(cs, w_ada, b_ada).reshape(depth, rows, 1, 6 * d)
    lat_row = lambda b: b
    ctx_row_of = lambda b: ctx_row

    cos2, sin2 = _rope_tables(s)
    cosc = jnp.zeros((cl, LANES), F32)
    head_of = jnp.arange(ATTN_W) // HEAD_DIM
    bd = (head_of[:, None] == head_of[None, :]).astype(BF16)

    tm_lat = min(ROW_TILE, s)
    tm_ctx = min(ROW_TILE, cl)
    xc = ctx
    out = routed = None
    for l in range(depth):
        last = l == depth - 1
        g1 = g_norm1[l].reshape(1, d)
        w_in_b = w_in[l].astype(BF16)
        gq = jnp.tile(g_q[l], N_Q_HEADS).reshape(1, ATTN_W)
        gk = jnp.tile(g_k[l], N_KV_HEADS).reshape(1, KV_W)
        conv_args = (w_dw[l], b_dw[l].reshape(1, -1), g_conv_ln[l].reshape(1, -1), b_conv_ln[l].reshape(1, -1))
        sgu_args = (g_sgu_ln[l].reshape(1, -1), b_sgu_ln[l].reshape(1, -1), w_s[l].astype(BF16),
                    jnp.repeat(b_s[l].T, CHUNK_W // CHUNK_HEADS, axis=1))
        wo_b = w_o[l].astype(BF16)
        g2 = g_norm2[l].reshape(1, d)

        if routed is None:
            q, kt, v, zc, zs = _proj(x, mod4, l, lat_row, g1, w_in_b, gq, gk, cos2, sin2, bd, rope=True, tm=tm_lat)
        else:
            x, q, kt, v, zc, zs = _combine_proj(*routed, x, mod4, l - 1, l, g1, w_in_b, gq, gk, cos2, sin2, bd,
                                                tm=tm_lat)
        if last:
            ktc, vc = _proj_kv(xc, mod4, l, ctx_row_of, g1, w_in_b[:, K0:CV0], gk, bd, tm=tm_ctx)
        else:
            qc, ktc, vc, zcc, zsc = _proj(xc, mod4, l, ctx_row_of, g1, w_in_b, gq, gk, cosc, cosc, bd,
                                          rope=False, tm=tm_ctx)
        a = _attention(q, [kt, ktc], [v, vc], tq=min(ATTN_Q_TILE, s))
        cv = _conv(zc, *conv_args, ts=min(CONV_TILE, s))
        ch = _sgu(zs, *sgu_args, tm=tm_lat)
        x, h2 = _oproj(a, cv, ch, x, mod4, l, lat_row, g2, wo_b, tm=tm_lat)
        tokens = h2.reshape(n_lat, d)
        if not last:
            ac = _attention(qc, [ktc], [vc], tq=min(ATTN_Q_TILE, cl))
            cvc = _conv(zcc, *conv_args, ts=min(CONV_TILE, cl))
            chc = _sgu(zsc, *sgu_args, tm=tm_ctx)
            xc, h2c = _oproj(ac, cvc, chc, xc, mod4, l, ctx_row_of, g2, wo_b, tm=tm_ctx)
            tokens = jnp.concatenate([tokens, h2c.reshape(n_ctx, d)], axis=0)

        ys, pos, wt = _moe(tokens, w_router[l].T, b_router[l].reshape(N_EXPERTS, 1), l,
                           w_gate, b_gate.reshape(depth, N_EXPERTS, 1, -1),
                           w_up, b_up.reshape(depth, N_EXPERTS, 1, -1),
                           w_down, b_down.reshape(depth, N_EXPERTS, 1, -1))
        tb = min(COMBINE_TILE, cl)
        pos_blocks = _block_layout(pos, tb)
        wt_tok = wt.T
        gf = g_final.reshape(1, d)
        if last:
            out = _combine(pos_blocks, ys, wt_tok, x, mod4, l, lat_row, gf, 0, tb=tb, final_norm=True)
        else:
            xc = _combine(pos_blocks, ys, wt_tok, xc, mod4, l, ctx_row_of, gf, n_lat // tb, tb=tb, final_norm=False)
            routed = (_block_layout(pos, tm_lat), ys, wt_tok)
    return out
```

```python
import functools
import math

import jax
import jax.numpy as jnp
from jax import lax
from jax.experimental import pallas as pl
from jax.experimental.pallas import tpu as pltpu

F32 = jnp.float32
BF16 = jnp.bfloat16
I32 = jnp.int32

D_MODEL = 1024
GRID_W = 64
EPS = 1e-6
HEAD_DIM = 64
ATTN_W = 512
N_Q_HEADS = 8
N_KV_HEADS = 2
Q_GROUP = 4
KV_W = 128
ROPE_THETA = 10000.0
AXIS_DIM = 32
CONV_W = 256
CONV_K = 31
CONV_HALO = 16
CHUNK_W = 256
CHUNK_HEADS = 4
CHUNK = 128
Q0, K0, V0, CV0, CH0, IN_W = 0, 512, 640, 768, 1280, 1792
N_EXPERTS = 32
TOP_K = 4
SWIGLU_LIMIT = 7.0
SWIGLU_ALPHA = 1.702
LOG2_E = 1.4426950408889634

LANES = 128
SUBLANES = 8
MXU_COLS = 256
VMEM_LIMIT = 56 * 1024 * 1024

ROW_TILE = 512
ATTN_Q_TILE = 256
ATTN_K_TILE = 512
CONV_TILE = 512
CONV_ROWS = 64
ROUTER_TILE = 1024
DISPATCH_TILE = 1024
EXPERT_TILE = 512
COMBINE_TILE = 256


def _fit(n, pref):
    return max(t for t in range(SUBLANES, min(n, pref) + 1, SUBLANES) if n % t == 0)


def _cparams(*sem):
    return pltpu.CompilerParams(dimension_semantics=sem, vmem_limit_bytes=VMEM_LIMIT)


def _split_bf16(a):
    hi = a.astype(BF16)
    lo = (a - hi.astype(F32)).astype(BF16)
    return hi, lo


def _dot(a, b):
    return jnp.dot(a, b, preferred_element_type=F32)


def _dot_nt(a, b):
    return lax.dot_general(a, b, (((1,), (1,)), ((), ())), preferred_element_type=F32)


def _dot3(a, b, dot=_dot):
    ah, al = _split_bf16(a)
    bh, bl = _split_bf16(b)
    return dot(ah, bh) + (dot(ah, bl) + dot(al, bh))


def _rmsnorm(x, g):
    return x * lax.rsqrt(jnp.mean(x * x, axis=-1, keepdims=True) + EPS) * g


def _layernorm(x, g, b):
    mu = jnp.mean(x, axis=-1, keepdims=True)
    xc = x - mu
    return xc * lax.rsqrt(jnp.mean(xc * xc, axis=-1, keepdims=True) + EPS) * g + b


def _ada_kernel(c_ref, w_ref, b_ref, o_ref):
    c = c_ref[...]
    s = c * jax.nn.sigmoid(c)
    o_ref[...] = _dot3(s, w_ref[...]) + b_ref[...]


def _ada(cs, w_ada, b_ada):
    depth, d, n6 = w_ada.shape
    r = cs.shape[0]
    tn = 1536
    return pl.pallas_call(
        _ada_kernel,
        out_shape=jax.ShapeDtypeStruct((depth, r, n6), F32),
        grid=(depth, n6 // tn),
        in_specs=[
            pl.BlockSpec((r, d), lambda l, j: (0, 0)),
            pl.BlockSpec((None, d, tn), lambda l, j: (l, 0, j)),
            pl.BlockSpec((None, 1, tn), lambda l, j: (l, 0, j)),
        ],
        out_specs=pl.BlockSpec((None, r, tn), lambda l, j: (l, 0, j)),
        compiler_params=_cparams("arbitrary", "arbitrary"),
        name="ada",
    )(cs, w_ada, b_ada.reshape(depth, 1, n6))


def _head_sumsq(t, bd):
    hi, lo = _split_bf16(t * t)
    return _dot(hi, bd) + _dot(lo, bd)


def _rope(t, cos, sin):
    w = t.shape[1]
    reps = w // LANES
    cosw = jnp.concatenate([cos] * reps, axis=1) if reps > 1 else cos
    sinw = jnp.concatenate([sin] * reps, axis=1) if reps > 1 else sin
    lane = lax.broadcasted_iota(I32, t.shape, 1)
    first = (lane % HEAD_DIM) < (HEAD_DIM // 2)
    rot = jnp.where(first, pltpu.roll(t, w - HEAD_DIM // 2, 1), pltpu.roll(t, HEAD_DIM // 2, 1))
    return t * cosw + rot * sinw


def _modulated(x, sh_ref, sc_ref, g_ref):
    return _rmsnorm(x, g_ref[...]) * (1.0 + sc_ref[...]) + sh_ref[...]


def _qk_head(t, g, bd, cos_ref, sin_ref, rope):
    t = t * lax.rsqrt(_head_sumsq(t, bd) * (1.0 / HEAD_DIM) + EPS) * g
    if rope:
        t = _rope(t, cos_ref[...], sin_ref[...])
    return t


PROJ_PHASES = 5


def _project(x, sh_ref, sc_ref, g1_ref, w_ref, gq_ref, gk_ref, cos_ref, sin_ref, bd_ref,
             q_ref, kt_ref, v_ref, zc_ref, zs_ref, rope, between=None):
    hook = between if between is not None else (lambda phase: None)
    hook(0)
    h = _modulated(x, sh_ref, sc_ref, g1_ref).astype(BF16)
    hook(1)
    q = _qk_head(_dot(h, w_ref[:, Q0:K0]), gq_ref[...], bd_ref[...], cos_ref, sin_ref, rope)
    q_ref[...] = (q * (HEAD_DIM ** -0.5 * LOG2_E)).astype(BF16)
    hook(2)
    kv = _dot(h, w_ref[:, K0:CV0])
    k = _qk_head(kv[:, 0:KV_W], gk_ref[...], bd_ref[0:KV_W, 0:KV_W], cos_ref, sin_ref, rope)
    kt_ref[...] = k.T.astype(BF16)
    v_ref[...] = kv[:, KV_W:].astype(BF16)
    hook(3)
    zc_ref[...] = _dot(h, w_ref[:, CV0:CH0])
    hook(4)
    zs_ref[...] = _dot(h, w_ref[:, CH0:IN_W])


def _proj_kernel(x_ref, *refs, rope):
    _project(x_ref[...], *refs, rope)


def _proj_kv_kernel(x_ref, sh_ref, sc_ref, g1_ref, w_ref, gk_ref, bd_ref, kt_ref, v_ref):
    h = _modulated(x_ref[...], sh_ref, sc_ref, g1_ref)
    p = _dot(h.astype(BF16), w_ref[...])
    k = _qk_head(p[:, 0:KV_W], gk_ref[...], bd_ref[0:KV_W, 0:KV_W], None, None, False)
    kt_ref[...] = k.T.astype(BF16)
    v_ref[...] = p[:, KV_W:2 * KV_W].astype(BF16)


def _mod_spec(layer, row_of, chunk):
    return pl.BlockSpec((None, None, 1, D_MODEL), lambda b, j: (layer, row_of(b), 0, chunk))


def _const_spec(shape):
    return pl.BlockSpec(shape, lambda b, j: tuple(0 for _ in shape))


def _proj(x, mod4, layer, row_of, g1, w_in_b, gq, gk, cos2, sin2, bd, *, rope, tm):
    bsz, s, d = x.shape
    grid = (bsz, s // tm)
    row = lambda w: pl.BlockSpec((None, tm, w), lambda b, j: (b, j, 0))
    return pl.pallas_call(
        functools.partial(_proj_kernel, rope=rope),
        out_shape=(
            jax.ShapeDtypeStruct((bsz, s, ATTN_W), BF16),
            jax.ShapeDtypeStruct((bsz, KV_W, s), BF16),
            jax.ShapeDtypeStruct((bsz, s, KV_W), BF16),
            jax.ShapeDtypeStruct((bsz, s, 2 * CONV_W), F32),
            jax.ShapeDtypeStruct((bsz, s, 2 * CHUNK_W), F32),
        ),
        grid=grid,
        in_specs=[
            row(d),
            _mod_spec(layer, row_of, 0),
            _mod_spec(layer, row_of, 1),
            _const_spec((1, d)),
            _const_spec((d, IN_W)),
            _const_spec((1, ATTN_W)),
            _const_spec((1, KV_W)),
            pl.BlockSpec((tm, LANES), lambda b, j: (j, 0)),
            pl.BlockSpec((tm, LANES), lambda b, j: (j, 0)),
            _const_spec((ATTN_W, ATTN_W)),
        ],
        out_specs=(
            row(ATTN_W),
            pl.BlockSpec((None, KV_W, tm), lambda b, j: (b, 0, j)),
            row(KV_W),
            row(2 * CONV_W),
            row(2 * CHUNK_W),
        ),
        compiler_params=_cparams("arbitrary", "arbitrary"),
        name="proj",
    )(x, mod4, mod4, g1, w_in_b, gq, gk, cos2, sin2, bd)


def _proj_kv(x, mod4, layer, row_of, g1, w_kv_b, gk, bd, *, tm):
    bsz, s, d = x.shape
    return pl.pallas_call(
        _proj_kv_kernel,
        out_shape=(
            jax.ShapeDtypeStruct((bsz, KV_W, s), BF16),
            jax.ShapeDtypeStruct((bsz, s, KV_W), BF16),
        ),
        grid=(bsz, s // tm),
        in_specs=[
            pl.BlockSpec((None, tm, d), lambda b, j: (b, j, 0)),
            _mod_spec(layer, row_of, 0),
            _mod_spec(layer, row_of, 1),
            _const_spec((1, d)),
            _const_spec((d, 2 * KV_W)),
            _const_spec((1, KV_W)),
            _const_spec((ATTN_W, ATTN_W)),
        ],
        out_specs=(
            pl.BlockSpec((None, KV_W, tm), lambda b, j: (b, 0, j)),
            pl.BlockSpec((None, tm, KV_W), lambda b, j: (b, j, 0)),
        ),
        compiler_params=_cparams("arbitrary", "arbitrary"),
        name="proj_kv",
    )(x, mod4, mod4, g1, w_kv_b, gk, bd)


def _attn_kernel(*refs, n_src):
    q_ref = refs[0]
    kt_refs = refs[1:1 + n_src]
    v_refs = refs[1 + n_src:1 + 2 * n_src]
    o_ref, s_ref, p_ref = refs[1 + 2 * n_src:]
    tiles, spans, col = [], [], 0
    for i, kt in enumerate(kt_refs):
        kn = kt.shape[1]
        tk = next(w for w in (ATTN_K_TILE, MXU_COLS, LANES) if kn % w == 0)
        tiles += [(i, o, col + o, tk) for o in range(0, kn, tk)]
        spans.append((col, kn))
        col += kn

    def scores_into(slot, h):
        g = h // Q_GROUP
        qh = q_ref[:, h * HEAD_DIM:(h + 1) * HEAD_DIM]
        mx = None
        for i, o, c, tk in tiles:
            tile = _dot(qh, kt_refs[i][g * HEAD_DIM:(g + 1) * HEAD_DIM, o:o + tk])
            s_ref[slot, :, c:c + tk] = tile
            for lo in range(0, tk, LANES):
                mx = tile[:, lo:lo + LANES] if mx is None else jnp.maximum(mx, tile[:, lo:lo + LANES])
        return jnp.max(mx, axis=-1, keepdims=True)

    m_next = scores_into(0, 0)
    outs = []
    for h in range(N_Q_HEADS):
        g, slot, m = h // Q_GROUP, h % 2, m_next
        if h + 1 < N_Q_HEADS:
            m_next = scores_into(1 - slot, h + 1)
        p = jnp.exp2(s_ref[slot] - m)
        denom = jnp.sum(p, axis=-1, keepdims=True)
        p_ref[slot] = p.astype(BF16)
        o = functools.reduce(jnp.add, [_dot(p_ref[slot, :, c:c + kn], v[...]) for (c, kn), v in zip(spans, v_refs)])
        outs.append(o[:, g * HEAD_DIM:(g + 1) * HEAD_DIM] / denom)
    o_ref[...] = jnp.concatenate(outs, axis=1).astype(BF16)


def _attention(q, kts, vs, *, tq):
    bsz, nq, _ = q.shape
    n_src = len(kts)
    kn_all = sum(kt.shape[2] for kt in kts)
    in_specs = [pl.BlockSpec((None, tq, ATTN_W), lambda b, j: (b, j, 0))]
    in_specs += [pl.BlockSpec((None, KV_W, kt.shape[2]), lambda b, j: (b, 0, 0)) for kt in kts]
    in_specs += [pl.BlockSpec((None, v.shape[1], KV_W), lambda b, j: (b, 0, 0)) for v in vs]
    return pl.pallas_call(
        functools.partial(_attn_kernel, n_src=n_src),
        out_shape=jax.ShapeDtypeStruct((bsz, nq, ATTN_W), BF16),
        grid=(bsz, nq // tq),
        in_specs=in_specs,
        out_specs=pl.BlockSpec((None, tq, ATTN_W), lambda b, j: (b, j, 0)),
        scratch_shapes=[pltpu.VMEM((2, tq, kn_all), F32), pltpu.VMEM((2, tq, kn_all), BF16)],
        compiler_params=_cparams("arbitrary", "arbitrary"),
        name="attention",
    )(q, *kts, *vs)


def _glu(z):
    return z[:, :CONV_W] * jax.nn.sigmoid(z[:, CONV_W:])


def _conv_kernel(zp_ref, z_ref, zn_ref, w_ref, bdw_ref, g_ref, b_ref, o_ref, u_ref, us_ref, *, ts):
    j = pl.program_id(1)
    nj = pl.num_programs(1)
    u_ref[0:CONV_HALO, :] = jnp.where(j > 0, _glu(zp_ref[...]), 0.0)
    u_ref[CONV_HALO + ts:CONV_HALO + ts + CONV_HALO, :] = jnp.where(j < nj - 1, _glu(zn_ref[...]), 0.0)
    u_ref[CONV_HALO:CONV_HALO + ts, :] = _glu(z_ref[...])
    rows = us_ref.shape[1]
    for b in range(1, SUBLANES):
        us_ref[b - 1] = u_ref[b:b + rows, :]
    off = CONV_HALO - CONV_K // 2
    for c in range(ts // CONV_ROWS):
        acc = jnp.zeros((CONV_ROWS, CONV_W), F32)
        for k in range(CONV_K):
            start = c * CONV_ROWS + off + k
            b, lo = start % SUBLANES, start - start % SUBLANES
            window = u_ref[lo:lo + CONV_ROWS, :] if b == 0 else us_ref[b - 1, lo:lo + CONV_ROWS, :]
            acc = acc + window * w_ref[k:k + 1, :]
        y = _layernorm(acc + bdw_ref[...], g_ref[...], b_ref[...])
        o_ref[c * CONV_ROWS:(c + 1) * CONV_ROWS, :] = (y * jax.nn.sigmoid(y)).astype(BF16)


def _conv(zc, w_dw, b_dw, g_ln, b_ln, *, ts):
    bsz, s, _ = zc.shape
    nh = ts // CONV_HALO
    last = s // CONV_HALO - 1
    return pl.pallas_call(
        functools.partial(_conv_kernel, ts=ts),
        out_shape=jax.ShapeDtypeStruct((bsz, s, CONV_W), BF16),
        grid=(bsz, s // ts),
        in_specs=[
            pl.BlockSpec((None, CONV_HALO, 2 * CONV_W), lambda b, j: (b, jnp.maximum(j * nh - 1, 0), 0)),
            pl.BlockSpec((None, ts, 2 * CONV_W), lambda b, j: (b, j, 0)),
            pl.BlockSpec((None, CONV_HALO, 2 * CONV_W), lambda b, j: (b, jnp.minimum((j + 1) * nh, last), 0)),
            _const_spec((CONV_K, CONV_W)),
            _const_spec((1, CONV_W)),
            _const_spec((1, CONV_W)),
            _const_spec((1, CONV_W)),
        ],
        out_specs=pl.BlockSpec((None, ts, CONV_W), lambda b, j: (b, j, 0)),
        scratch_shapes=[pltpu.VMEM((ts + 2 * CONV_HALO, CONV_W), F32),
                        pltpu.VMEM((SUBLANES - 1, ts + 2 * CONV_HALO - SUBLANES, CONV_W), F32)],
        compiler_params=_cparams("arbitrary", "arbitrary"),
        name="conv",
    )(zc, zc, zc, w_dw, b_dw, g_ln, b_ln)


def _sgu_kernel(z_ref, g_ref, b_ref, ws_ref, bs_ref, o_ref, *, tm):
    z = z_ref[...]
    z = 0.5 * z * (1.0 + lax.erf(z * (2.0 ** -0.5)))
    u = z[:, :CHUNK_W]
    v = _layernorm(z[:, CHUNK_W:], g_ref[...], b_ref[...]).astype(BF16)
    head = lax.broadcasted_iota(I32, (CHUNK, CHUNK_W), 1) // (CHUNK_W // CHUNK_HEADS)
    for c in range(tm // CHUNK):
        vc = v[c * CHUNK:(c + 1) * CHUNK, :]
        s = bs_ref[...]
        for h in range(CHUNK_HEADS):
            s = s + jnp.where(head == h, _dot(ws_ref[h], vc), 0.0)
        o_ref[c * CHUNK:(c + 1) * CHUNK, :] = (u[c * CHUNK:(c + 1) * CHUNK, :] * s).astype(BF16)


def _sgu(zs, g_ln, b_ln, ws_b, bs_full, *, tm):
    bsz, s, _ = zs.shape
    return pl.pallas_call(
        functools.partial(_sgu_kernel, tm=tm),
        out_shape=jax.ShapeDtypeStruct((bsz, s, CHUNK_W), BF16),
        grid=(bsz, s // tm),
        in_specs=[
            pl.BlockSpec((None, tm, 2 * CHUNK_W), lambda b, j: (b, j, 0)),
            _const_spec((1, CHUNK_W)),
            _const_spec((1, CHUNK_W)),
            _const_spec((CHUNK_HEADS, CHUNK, CHUNK)),
            _const_spec((CHUNK, CHUNK_W)),
        ],
        out_specs=pl.BlockSpec((None, tm, CHUNK_W), lambda b, j: (b, j, 0)),
        compiler_params=_cparams("arbitrary", "arbitrary"),
        name="sgu",
    )(zs, g_ln, b_ln, ws_b, bs_full)


def _oproj_kernel(a_ref, cv_ref, ch_ref, x_ref, gt_ref, sh_ref, sc_ref, g2_ref, wo_ref, xo_ref, h2_ref):
    y = (_dot(a_ref[...], wo_ref[0:ATTN_W, :])
         + _dot(cv_ref[...], wo_ref[ATTN_W:ATTN_W + CONV_W, :])
         + _dot(ch_ref[...], wo_ref[ATTN_W + CONV_W:, :]))
    xn = x_ref[...] + gt_ref[...] * y
    xo_ref[...] = xn
    h2_ref[...] = _rmsnorm(xn, g2_ref[...]) * (1.0 + sc_ref[...]) + sh_ref[...]


def _oproj(a, cv, ch, x, mod4, layer, row_of, g2, wo_b, *, tm):
    bsz, s, d = x.shape
    row = lambda w: pl.BlockSpec((None, tm, w), lambda b, j: (b, j, 0))
    return pl.pallas_call(
        _oproj_kernel,
        out_shape=(jax.ShapeDtypeStruct((bsz, s, d), F32), jax.ShapeDtypeStruct((bsz, s, d), F32)),
        grid=(bsz, s // tm),
        in_specs=[
            row(ATTN_W), row(CONV_W), row(CHUNK_W), row(d),
            _mod_spec(layer, row_of, 2),
            _mod_spec(layer, row_of, 3),
            _mod_spec(layer, row_of, 4),
            _const_spec((1, d)),
            _const_spec((d, d)),
        ],
        out_specs=(row(d), row(d)),
        compiler_params=_cparams("arbitrary", "arbitrary"),
        name="oproj",
    )(a, cv, ch, x, mod4, mod4, mod4, g2, wo_b)


def _router_kernel(*refs, tb, first_blocks):
    n_src = 1 if first_blocks is None else 2
    h_refs = refs[:n_src]
    wr_ref, br_ref, idx_ref, wt_ref, rank_ref, cnt_ref, carry_ref = refs[n_src:]

    @pl.when(pl.program_id(0) == 0)
    def _():
        carry_ref[...] = jnp.zeros_like(carry_ref)

    h = h_refs[0][...] if n_src == 1 else jnp.where(pl.program_id(0) < first_blocks, h_refs[0][...], h_refs[1][...])
    logits = _dot3(wr_ref[...], h, dot=_dot_nt) + br_ref[...]
    e_iota = lax.broadcasted_iota(I32, (N_EXPERTS, tb), 0)
    vals, idxs = [], []
    work = logits
    for _ in range(TOP_K):
        m = jnp.max(work, axis=0, keepdims=True)
        ik = jnp.min(jnp.where(work == m, e_iota, N_EXPERTS), axis=0, keepdims=True)
        vals.append(m)
        idxs.append(ik)
        work = jnp.where(e_iota == ik, -jnp.inf, work)
    exps = [jnp.exp(v - vals[0]) for v in vals]
    denom = functools.reduce(jnp.add, exps)
    wt_ref[...] = jnp.concatenate([e / denom for e in exps], axis=0)
    idx_ref[...] = jnp.concatenate(idxs, axis=0)

    onehot = functools.reduce(jnp.add, [(e_iota == ik).astype(F32) for ik in idxs])
    upper = (lax.broadcasted_iota(I32, (tb, tb), 0) < lax.broadcasted_iota(I32, (tb, tb), 1)).astype(BF16)
    before = _dot(onehot.astype(BF16), upper) + carry_ref[...]
    ranks = [jnp.sum(jnp.where(e_iota == ik, before, 0.0), axis=0, keepdims=True) for ik in idxs]
    rank_ref[...] = jnp.concatenate(ranks, axis=0).astype(I32)
    carry_ref[...] = carry_ref[...] + jnp.sum(onehot, axis=1, keepdims=True)
    cnt_ref[...] = jnp.broadcast_to(carry_ref[...], cnt_ref.shape)


def _source_specs(hs, tb):
    d = hs[0].shape[1]
    if len(hs) == 1:
        return [pl.BlockSpec((tb, d), lambda i, *_: (i, 0))], None
    nl = hs[0].shape[0] // tb
    return [pl.BlockSpec((tb, d), lambda i, *_: (jnp.minimum(i, nl - 1), 0)),
            pl.BlockSpec((tb, d), lambda i, *_: (jnp.maximum(i - nl, 0), 0))], nl


def _router(hs, wr_t, br, *, tb):
    n, d = sum(h.shape[0] for h in hs), hs[0].shape[1]
    tok = lambda dt: jax.ShapeDtypeStruct((TOP_K, n), dt)
    tok_spec = pl.BlockSpec((TOP_K, tb), lambda i: (0, i))
    src_specs, first_blocks = _source_specs(hs, tb)
    return pl.pallas_call(
        functools.partial(_router_kernel, tb=tb, first_blocks=first_blocks),
        out_shape=(tok(I32), tok(F32), tok(I32), jax.ShapeDtypeStruct((N_EXPERTS, LANES), F32)),
        grid=(n // tb,),
        in_specs=src_specs + [
            pl.BlockSpec((N_EXPERTS, d), lambda i: (0, 0)),
            pl.BlockSpec((N_EXPERTS, 1), lambda i: (0, 0)),
        ],
        out_specs=(tok_spec, tok_spec, tok_spec, pl.BlockSpec((N_EXPERTS, LANES), lambda i: (0, 0))),
        scratch_shapes=[pltpu.VMEM((N_EXPERTS, 1), F32)],
        compiler_params=_cparams("arbitrary"),
        name="router",
    )(*hs, wr_t, br)


def _pos_kernel(start_ref, idx_ref, rank_ref, pos_ref):
    idx = idx_ref[...]
    pos = rank_ref[...]
    for e in range(N_EXPERTS):
        pos = pos + jnp.where(idx == e, start_ref[e], 0)
    pos_ref[...] = pos


def _positions(starts, idx, rank):
    return pl.pallas_call(
        _pos_kernel,
        out_shape=jax.ShapeDtypeStruct(idx.shape, I32),
        grid_spec=pltpu.PrefetchScalarGridSpec(
            num_scalar_prefetch=1, grid=(1,),
            in_specs=[pl.BlockSpec(idx.shape, lambda i, s: (0, 0)), pl.BlockSpec(idx.shape, lambda i, s: (0, 0))],
            out_specs=pl.BlockSpec(idx.shape, lambda i, s: (0, 0))),
        compiler_params=_cparams("arbitrary"),
        name="positions",
    )(starts, idx, rank)


def _row_copy(src, src_row, dst, dst_row, sem):
    return pltpu.make_async_copy(src.at[pl.ds(src_row, 1)], dst.at[pl.ds(dst_row, 1)], sem)


def _dispatch_kernel(zf_ref, pos_ref, *refs, tb, tm, nblk, first_blocks):
    n_src = 1 if first_blocks is None else 2
    h_refs = refs[:n_src]
    xs_ref, zbuf_ref, sem, zsem = refs[n_src:]

    @pl.when(pl.program_id(0) == 0)
    def _():
        zbuf_ref[...] = jnp.zeros_like(zbuf_ref)

        def zero_copy(i):
            return pltpu.make_async_copy(zbuf_ref, xs_ref.at[pl.ds(pl.multiple_of(i * tm, tm), tm)], zsem)

        def start(i, carry):
            @pl.when(zf_ref[i] != 0)
            def _():
                zero_copy(i).start()
            return carry

        def wait(i, carry):
            @pl.when(zf_ref[i] != 0)
            def _():
                zero_copy(i).wait()
            return carry

        lax.fori_loop(0, nblk, start, 0)
        lax.fori_loop(0, nblk, wait, 0)

    def scatter_rows(h_ref):
        def issue(t8, carry):
            t0 = pl.multiple_of(t8 * SUBLANES, SUBLANES)
            for r in range(SUBLANES):
                for k in range(TOP_K):
                    _row_copy(h_ref, t0 + r, xs_ref, pos_ref[0, k * tb + t0 + r], sem).start(priority=(r + k) % 2)
            return carry

        lax.fori_loop(0, tb // SUBLANES, issue, 0)

    if n_src == 1:
        scatter_rows(h_refs[0])
    else:
        pl.when(pl.program_id(0) < first_blocks)(lambda: scatter_rows(h_refs[0]))
        pl.when(pl.program_id(0) >= first_blocks)(lambda: scatter_rows(h_refs[1]))
    pltpu.make_async_copy(xs_ref.at[pl.ds(0, TOP_K * tb)], xs_ref.at[pl.ds(0, TOP_K * tb)], sem).wait()


def _dispatch(zero_flag, pos_blocks, hs, *, tb, tm):
    n, d = sum(h.shape[0] for h in hs), hs[0].shape[1]
    nblk = zero_flag.shape[0]
    src_specs, first_blocks = _source_specs(hs, tb)
    return pl.pallas_call(
        functools.partial(_dispatch_kernel, tb=tb, tm=tm, nblk=nblk, first_blocks=first_blocks),
        out_shape=jax.ShapeDtypeStruct((nblk * tm, d), F32),
        grid_spec=pltpu.PrefetchScalarGridSpec(
            num_scalar_prefetch=1, grid=(n // tb,),
            in_specs=[pl.BlockSpec((None, 1, TOP_K * tb), lambda i, zf: (i, 0, 0), memory_space=pltpu.SMEM)]
            + src_specs,
            out_specs=pl.BlockSpec(memory_space=pl.ANY),
            scratch_shapes=[pltpu.VMEM((tm, d), F32), pltpu.SemaphoreType.DMA(()), pltpu.SemaphoreType.DMA(())]),
        compiler_params=_cparams("arbitrary"),
        name="dispatch",
    )(zero_flag, pos_blocks, *hs)


def _expert_kernel(be_ref, ne_ref, nb_ref, xs_ref, wg_ref, bg_ref, wu_ref, bu_ref, wd_ref, bd_ref, ys_ref,
                   wf_ref, wb_ref, sems, *, layer):
    i = pl.program_id(0)
    used = i < nb_ref[0]
    new_expert = jnp.logical_or(i == 0, be_ref[i] != be_ref[jnp.maximum(i - 1, 0)])

    def fetch(e):
        return [pltpu.make_async_copy(w.at[layer, e], wf_ref.at[m], sems.at[m])
                for m, w in enumerate((wg_ref, wu_ref, wd_ref))]

    @pl.when(i == 0)
    def _():
        for cp in fetch(be_ref[0]):
            cp.start()

    @pl.when(jnp.logical_and(used, new_expert))
    def _():
        for m, cp in enumerate(fetch(be_ref[i])):
            cp.wait()
            wb_ref[m] = wf_ref[m].astype(BF16)
        nxt = ne_ref[i]

        @pl.when(nxt >= 0)
        def _():
            for cp in fetch(nxt):
                cp.start()

    @pl.when(used)
    def _():
        x = xs_ref[...].astype(BF16)
        g = jnp.minimum(_dot(x, wb_ref[0]) + bg_ref[...], SWIGLU_LIMIT)
        u = jnp.clip(_dot(x, wb_ref[1]) + bu_ref[...], -SWIGLU_LIMIT, SWIGLU_LIMIT)
        a = (u + 1.0) * (g * jax.nn.sigmoid(SWIGLU_ALPHA * g))
        ys_ref[...] = _dot(a.astype(BF16), wb_ref[2]) + bd_ref[...]

    @pl.when(jnp.logical_not(used))
    def _():
        ys_ref[...] = jnp.zeros_like(ys_ref)


def _experts(block_e, next_e, n_used, xs, layer, wg, bg, wu, bu, wd, bd, *, tm):
    p, d = xs.shape
    assert wg.shape[2:] == (d, d) and wd.shape[2:] == (d, d)
    bias = pl.BlockSpec((None, None, 1, d), lambda i, be, ne, nb: (layer, be[i], 0, 0))
    hbm = pl.BlockSpec(memory_space=pl.ANY)
    return pl.pallas_call(
        functools.partial(_expert_kernel, layer=layer),
        out_shape=jax.ShapeDtypeStruct((p, d), F32),
        grid_spec=pltpu.PrefetchScalarGridSpec(
            num_scalar_prefetch=3, grid=(p // tm,),
            in_specs=[pl.BlockSpec((tm, d), lambda i, be, ne, nb: (i, 0)), hbm, bias, hbm, bias, hbm, bias],
            out_specs=pl.BlockSpec((tm, d), lambda i, be, ne, nb: (i, 0)),
            scratch_shapes=[pltpu.VMEM((3, d, d), F32), pltpu.VMEM((3, d, d), BF16), pltpu.SemaphoreType.DMA((3,))]),
        compiler_params=_cparams("arbitrary"),
        name="experts",
    )(block_e, next_e, n_used, xs, wg, bg, wu, bu, wd, bd)


def _combine_kernel(pos_ref, posn_ref, ys_ref, wt_ref, x_ref, gt_ref, gf_ref, o_ref, buf_ref, sems, *, tb, final_norm):
    i = pl.program_id(0)
    n = pl.num_programs(0)

    def gather(p_ref, slot):
        def issue(t8, carry):
            t0 = pl.multiple_of(t8 * SUBLANES, SUBLANES)
            for r in range(SUBLANES):
                for k in range(TOP_K):
                    _row_copy(ys_ref, p_ref[0, k * tb + t0 + r], buf_ref.at[slot, k], t0 + r,
                              sems.at[slot]).start(priority=(r + k) % 2)
            return carry

        lax.fori_loop(0, tb // SUBLANES, issue, 0)

    @pl.when(i == 0)
    def _():
        gather(pos_ref, 0)

    @pl.when(i + 1 < n)
    def _():
        gather(posn_ref, (i + 1) % 2)

    slot = i % 2
    for k in range(TOP_K):
        pltpu.make_async_copy(ys_ref.at[pl.ds(0, tb)], buf_ref.at[slot, k], sems.at[slot]).wait()
    m = functools.reduce(jnp.add, [buf_ref[slot, k] * wt_ref[:, k:k + 1] for k in range(TOP_K)])
    xn = x_ref[...] + gt_ref[...] * m
    o_ref[...] = _rmsnorm(xn, gf_ref[...]) if final_norm else xn


def _combine(pos_blocks, ys, wt_tok, x, mod4, layer, row_of, g_final, blk0, *, tb, final_norm):
    bsz, s, d = x.shape
    nj = s // tb
    steps = bsz * nj
    return pl.pallas_call(
        functools.partial(_combine_kernel, tb=tb, final_norm=final_norm),
        out_shape=jax.ShapeDtypeStruct((bsz, s, d), F32),
        grid=(steps,),
        in_specs=[
            pl.BlockSpec((None, 1, TOP_K * tb), lambda i: (blk0 + i, 0, 0), memory_space=pltpu.SMEM),
            pl.BlockSpec((None, 1, TOP_K * tb), lambda i: (blk0 + jnp.minimum(i + 1, steps - 1), 0, 0),
                         memory_space=pltpu.SMEM),
            pl.BlockSpec(memory_space=pl.ANY),
            pl.BlockSpec((tb, TOP_K), lambda i: (blk0 + i, 0)),
            pl.BlockSpec((None, tb, d), lambda i: (i // nj, i % nj, 0)),
            pl.BlockSpec((None, None, 1, d), lambda i: (layer, row_of(i // nj), 0, 5)),
            pl.BlockSpec((1, d), lambda i: (0, 0)),
        ],
        out_specs=pl.BlockSpec((None, tb, d), lambda i: (i // nj, i % nj, 0)),
        scratch_shapes=[pltpu.VMEM((2, TOP_K, tb, d), F32), pltpu.SemaphoreType.DMA((2,))],
        compiler_params=_cparams("arbitrary"),
        name="combine",
    )(pos_blocks, pos_blocks, ys, wt_tok, x, mod4, g_final)


def _combine_proj_kernel(pos_ref, posn_ref, ys_ref, wt_ref, x_ref, gt_ref, *refs, tb):
    proj_refs, xo_ref, out_refs, buf_ref, sems = refs[:9], refs[9], refs[10:15], refs[15], refs[16]
    i = pl.program_id(0)
    n = pl.num_programs(0)

    def drain(slot):
        for k in range(TOP_K):
            pltpu.make_async_copy(ys_ref.at[pl.ds(0, tb)], buf_ref.at[slot, k], sems.at[slot]).wait()

    @pl.when(i == 0)
    def _():
        def issue(t, carry):
            for k in range(TOP_K):
                _row_copy(ys_ref, pos_ref[0, k * tb + t], buf_ref.at[0, k], t, sems.at[0]).start(priority=k % 2)
            return carry

        lax.fori_loop(0, tb, issue, 0)

    slot = i % 2
    drain(slot)
    m = functools.reduce(jnp.add, [buf_ref[slot, k] * wt_ref[:, k:k + 1] for k in range(TOP_K)])
    xn = x_ref[...] + gt_ref[...] * m
    xo_ref[...] = xn

    def gather_slice(phase):
        for t in range(phase * tb // PROJ_PHASES, (phase + 1) * tb // PROJ_PHASES):
            for k in range(TOP_K):
                _row_copy(ys_ref, posn_ref[0, k * tb + t], buf_ref.at[1 - slot, k], t,
                          sems.at[1 - slot]).start(priority=(t + k) % 2)

    _project(xn, *proj_refs, *out_refs, True, between=gather_slice)

    @pl.when(i == n - 1)
    def _():
        drain(1 - slot)


def _combine_proj(pos_blocks, ys, wt_tok, x, mod4, prev_layer, layer, g1, w_in_b, gq, gk, cos2, sin2, bd, *, tm):
    bsz, s, d = x.shape
    nj = s // tm
    steps = bsz * nj
    row = lambda w: pl.BlockSpec((None, tm, w), lambda i: (i // nj, i % nj, 0))
    mod = lambda l, c: pl.BlockSpec((None, None, 1, d), lambda i: (l, i // nj, 0, c))
    const = lambda shape: pl.BlockSpec(shape, lambda i: tuple(0 for _ in shape))
    smem = lambda f: pl.BlockSpec((None, 1, TOP_K * tm), lambda i: (f(i), 0, 0), memory_space=pltpu.SMEM)
    return pl.pallas_call(
        functools.partial(_combine_proj_kernel, tb=tm),
        out_shape=(
            jax.ShapeDtypeStruct((bsz, s, d), F32),
            jax.ShapeDtypeStruct((bsz, s, ATTN_W), BF16),
            jax.ShapeDtypeStruct((bsz, KV_W, s), BF16),
            jax.ShapeDtypeStruct((bsz, s, KV_W), BF16),
            jax.ShapeDtypeStruct((bsz, s, 2 * CONV_W), F32),
            jax.ShapeDtypeStruct((bsz, s, 2 * CHUNK_W), F32),
        ),
        grid=(steps,),
        in_specs=[
            smem(lambda i: i),
            smem(lambda i: jnp.minimum(i + 1, steps - 1)),
            pl.BlockSpec(memory_space=pl.ANY),
            pl.BlockSpec((tm, TOP_K), lambda i: (i, 0)),
            row(d),
            mod(prev_layer, 5),
            mod(layer, 0),
            mod(layer, 1),
            const((1, d)),
            const((d, IN_W)),
            const((1, ATTN_W)),
            const((1, KV_W)),
            pl.BlockSpec((tm, LANES), lambda i: (i % nj, 0)),
            pl.BlockSpec((tm, LANES), lambda i: (i % nj, 0)),
            const((ATTN_W, ATTN_W)),
        ],
        out_specs=(
            row(d),
            row(ATTN_W),
            pl.BlockSpec((None, KV_W, tm), lambda i: (i // nj, 0, i % nj)),
            row(KV_W),
            row(2 * CONV_W),
            row(2 * CHUNK_W),
        ),
        scratch_shapes=[pltpu.VMEM((2, TOP_K, tm, d), F32), pltpu.SemaphoreType.DMA((2,))],
        compiler_params=_cparams("arbitrary"),
        name="combine_proj",
    )(pos_blocks, pos_blocks, ys, wt_tok, x, mod4, mod4, mod4, g1, w_in_b, gq, gk, cos2, sin2, bd)


def _block_layout(pos, tb):
    n = pos.shape[1]
    return pos.reshape(TOP_K, n // tb, tb).transpose(1, 0, 2).reshape(n // tb, 1, TOP_K * tb)


def _moe(hs, wr_t, br, layer, wg, bg, wu, bu, wd, bd):
    n = sum(h.shape[0] for h in hs)
    tm = EXPERT_TILE
    common = functools.reduce(math.gcd, [h.shape[0] for h in hs])
    td = _fit(common, DISPATCH_TILE)
    idx, wt, rank, cnt = _router(hs, wr_t, br, tb=_fit(common, ROUTER_TILE))
    counts = cnt[:, 0].astype(I32)
    padded = (counts + tm - 1) // tm * tm
    ends = jnp.cumsum(padded)
    pos = _positions((ends - padded).astype(I32), idx, rank)
    nblk = (n * TOP_K) // tm + N_EXPERTS
    n_used = ends[-1] // tm
    blk = jnp.arange(nblk, dtype=I32)
    first_row = jnp.minimum(blk, n_used - 1) * tm
    block_e = jnp.minimum(jnp.sum(ends[None, :] <= first_row[:, None], axis=1), N_EXPERTS - 1).astype(I32)
    partial_or_unused = (blk >= n_used) | jnp.any(((blk + 1) * tm)[:, None] == ends[None, :], axis=1)
    xs = _dispatch(partial_or_unused.astype(I32), _block_layout(pos, td), hs, tb=td, tm=tm)
    e_ids = jnp.arange(N_EXPERTS, dtype=I32)
    later = (e_ids[None, :] > e_ids[:, None]) & (padded[None, :] > 0)
    next_nonempty = jnp.min(jnp.where(later, e_ids[None, :], N_EXPERTS), axis=1)
    next_e = jnp.where(next_nonempty < N_EXPERTS, next_nonempty, -1).astype(I32)[block_e]
    ys = _experts(block_e, next_e, n_used.reshape(1).astype(I32), xs, layer, wg, bg, wu, bu, wd, bd, tm=tm)
    return ys, pos, wt


def _rope_tables(s):
    pos = jnp.arange(s)
    r, col = (pos // GRID_W).astype(F32), (pos % GRID_W).astype(F32)
    inv = ROPE_THETA ** (-jnp.arange(0, AXIS_DIM, 2, dtype=F32) / AXIS_DIM)
    ang = jnp.concatenate([r[:, None] * inv, col[:, None] * inv], axis=-1)
    cos, sin = jnp.cos(ang), jnp.sin(ang)
    return jnp.tile(jnp.concatenate([cos, cos], axis=-1), (1, 2)), jnp.tile(jnp.concatenate([-sin, sin], axis=-1), (1, 2))


def kernel(x, c, ctx, c_ctx, w_ada, b_ada, g_norm1, w_in, g_q, g_k, w_dw, b_dw, g_conv_ln, b_conv_ln,
           g_sgu_ln, b_sgu_ln, w_s, b_s, w_o, g_norm2, w_router, b_router, w_gate, b_gate, w_up, b_up,
           w_down, b_down, g_final):
    bsz, s, d = x.shape
    cl = ctx.shape[1]
    depth = w_ada.shape[0]
    n_lat, n_ctx = bsz * s, bsz * cl
    ctx_row = 8 * ((bsz + 7) // 8)
    rows = ctx_row + 8

    cs = jnp.zeros((rows, d), F32).at[:bsz].set(c).at[ctx_row].set(c_ctx)
    mod4 = _ada(cs, w_ada, b_ada).reshape(depth, rows, 1, 6 * d)
    lat_row = lambda b: b
    ctx_row_of = lambda b: ctx_row

    cos2, sin2 = _rope_tables(s)
    cosc = jnp.zeros((cl, LANES), F32)
    head_of = jnp.arange(ATTN_W) // HEAD_DIM
    bd = (head_of[:, None] == head_of[None, :]).astype(BF16)

    tm_lat = min(ROW_TILE, s)
    tm_ctx = min(ROW_TILE, cl)
    xc = ctx
    out = routed = None
    for l in range(depth):
        last = l == depth - 1
        g1 = g_norm1[l].reshape(1, d)
        w_in_b = w_in[l].astype(BF16)
        gq = jnp.tile(g_q[l], N_Q_HEADS).reshape(1, ATTN_W)
        gk = jnp.tile(g_k[l], N_KV_HEADS).reshape(1, KV_W)
        conv_args = (w_dw[l], b_dw[l].reshape(1, -1), g_conv_ln[l].reshape(1, -1), b_conv_ln[l].reshape(1, -1))
        sgu_args = (g_sgu_ln[l].reshape(1, -1), b_sgu_ln[l].reshape(1, -1), w_s[l].astype(BF16),
                    jnp.repeat(b_s[l].T, CHUNK_W // CHUNK_HEADS, axis=1))
        wo_b = w_o[l].astype(BF16)
        g2 = g_norm2[l].reshape(1, d)

        if routed is None:
            q, kt, v, zc, zs = _proj(x, mod4, l, lat_row, g1, w_in_b, gq, gk, cos2, sin2, bd, rope=True, tm=tm_lat)
        else:
            x, q, kt, v, zc, zs = _combine_proj(*routed, x, mod4, l - 1, l, g1, w_in_b, gq, gk, cos2, sin2, bd,
                                                tm=tm_lat)
        if last:
            ktc, vc = _proj_kv(xc, mod4, l, ctx_row_of, g1, w_in_b[:, K0:CV0], gk, bd, tm=tm_ctx)
        else:
            qc, ktc, vc, zcc, zsc = _proj(xc, mod4, l, ctx_row_of, g1, w_in_b, gq, gk, cosc, cosc, bd,
                                          rope=False, tm=tm_ctx)
        a = _attention(q, [kt, ktc], [v, vc], tq=min(ATTN_Q_TILE, s))
        cv = _conv(zc, *conv_args, ts=min(CONV_TILE, s))
        ch = _sgu(zs, *sgu_args, tm=tm_lat)
        x, h2 = _oproj(a, cv, ch, x, mod4, l, lat_row, g2, wo_b, tm=tm_lat)
        tokens = [h2.reshape(n_lat, d)]
        if not last:
            ac = _attention(qc, [ktc], [vc], tq=min(ATTN_Q_TILE, cl))
            cvc = _conv(zcc, *conv_args, ts=min(CONV_TILE, cl))
            chc = _sgu(zsc, *sgu_args, tm=tm_ctx)
            xc, h2c = _oproj(ac, cvc, chc, xc, mod4, l, ctx_row_of, g2, wo_b, tm=tm_ctx)
            tokens.append(h2c.reshape(n_ctx, d))

        ys, pos, wt = _moe(tokens, w_router[l].T, b_router[l].reshape(N_EXPERTS, 1), l,
                           w_gate, b_gate.reshape(depth, N_EXPERTS, 1, -1),
                           w_up, b_up.reshape(depth, N_EXPERTS, 1, -1),
                           w_down, b_down.reshape(depth, N_EXPERTS, 1, -1))
        tb = min(COMBINE_TILE, cl)
        pos_blocks = _block_layout(pos, tb)
        wt_tok = wt.T
        gf = g_final.reshape(1, d)
        if last:
            out = _combine(pos_blocks, ys, wt_tok, x, mod4, l, lat_row, gf, 0, tb=tb, final_norm=True)
        else:
            xc = _combine(pos_blocks, ys, wt_tok, xc, mod4, l, ctx_row_of, gf, n_lat // tb, tb=tb, final_norm=False)
            routed = (_block_layout(pos, tm_lat), ys, wt_tok)
    return out
```

```python
import functools
import math

import jax
import jax.numpy as jnp
from jax import lax
from jax.experimental import pallas as pl
from jax.experimental.pallas import tpu as pltpu

F32 = jnp.float32
BF16 = jnp.bfloat16
I32 = jnp.int32

D_MODEL = 1024
GRID_W = 64
EPS = 1e-6
HEAD_DIM = 64
ATTN_W = 512
N_Q_HEADS = 8
N_KV_HEADS = 2
Q_GROUP = 4
KV_W = 128
ROPE_THETA = 10000.0
AXIS_DIM = 32
CONV_W = 256
CONV_K = 31
CONV_HALO = 16
CHUNK_W = 256
CHUNK_HEADS = 4
CHUNK = 128
Q0, K0, V0, CV0, CH0, IN_W = 0, 512, 640, 768, 1280, 1792
N_EXPERTS = 32
TOP_K = 4
SWIGLU_LIMIT = 7.0
SWIGLU_ALPHA = 1.702
LOG2_E = 1.4426950408889634

LANES = 128
SUBLANES = 8
MXU_COLS = 256
VMEM_LIMIT = 56 * 1024 * 1024

ROW_TILE = 512
ATTN_Q_TILE = 256
ATTN_K_TILE = 512
CONV_TILE = 512
CONV_ROWS = 64
ROUTER_TILE = 1024
DISPATCH_TILE = 2048
EXPERT_TILE = 512
COMBINE_TILE = 256
FINAL_COMBINE_TILE = 512


def _fit(n, pref):
    return max(t for t in range(SUBLANES, min(n, pref) + 1, SUBLANES) if n % t == 0)


def _cparams(*sem):
    return pltpu.CompilerParams(dimension_semantics=sem, vmem_limit_bytes=VMEM_LIMIT)


def _split_bf16(a):
    hi = a.astype(BF16)
    lo = (a - hi.astype(F32)).astype(BF16)
    return hi, lo


def _dot(a, b):
    return jnp.dot(a, b, preferred_element_type=F32)


def _dot_nt(a, b):
    return lax.dot_general(a, b, (((1,), (1,)), ((), ())), preferred_element_type=F32)


def _dot3(a, b, dot=_dot):
    ah, al = _split_bf16(a)
    bh, bl = _split_bf16(b)
    return dot(ah, bh) + (dot(ah, bl) + dot(al, bh))


def _rmsnorm(x, g):
    return x * lax.rsqrt(jnp.mean(x * x, axis=-1, keepdims=True) + EPS) * g


def _layernorm(x, g, b):
    mu = jnp.mean(x, axis=-1, keepdims=True)
    xc = x - mu
    return xc * lax.rsqrt(jnp.mean(xc * xc, axis=-1, keepdims=True) + EPS) * g + b


def _ada_kernel(c_ref, w_ref, b_ref, o_ref):
    c = c_ref[...]
    s = c * jax.nn.sigmoid(c)
    o_ref[...] = _dot3(s, w_ref[...]) + b_ref[...]


def _ada(cs, w_ada, b_ada):
    depth, d, n6 = w_ada.shape
    r = cs.shape[0]
    tn = 1536
    return pl.pallas_call(
        _ada_kernel,
        out_shape=jax.ShapeDtypeStruct((depth, r, n6), F32),
        grid=(depth, n6 // tn),
        in_specs=[
            pl.BlockSpec((r, d), lambda l, j: (0, 0)),
            pl.BlockSpec((None, d, tn), lambda l, j: (l, 0, j)),
            pl.BlockSpec((None, 1, tn), lambda l, j: (l, 0, j)),
        ],
        out_specs=pl.BlockSpec((None, r, tn), lambda l, j: (l, 0, j)),
        compiler_params=_cparams("arbitrary", "arbitrary"),
        name="ada",
    )(cs, w_ada, b_ada.reshape(depth, 1, n6))


def _head_sumsq(t, bd):
    hi, lo = _split_bf16(t * t)
    return _dot(hi, bd) + _dot(lo, bd)


def _rope(t, cos, sin):
    w = t.shape[1]
    reps = w // LANES
    cosw = jnp.concatenate([cos] * reps, axis=1) if reps > 1 else cos
    sinw = jnp.concatenate([sin] * reps, axis=1) if reps > 1 else sin
    lane = lax.broadcasted_iota(I32, t.shape, 1)
    first = (lane % HEAD_DIM) < (HEAD_DIM // 2)
    rot = jnp.where(first, pltpu.roll(t, w - HEAD_DIM // 2, 1), pltpu.roll(t, HEAD_DIM // 2, 1))
    return t * cosw + rot * sinw


def _modulated(x, sh_ref, sc_ref, g_ref):
    return _rmsnorm(x, g_ref[...]) * (1.0 + sc_ref[...]) + sh_ref[...]


def _qk_head(t, g, bd, cos_ref, sin_ref, rope):
    t = t * lax.rsqrt(_head_sumsq(t, bd) * (1.0 / HEAD_DIM) + EPS) * g
    if rope:
        t = _rope(t, cos_ref[...], sin_ref[...])
    return t


PROJ_PHASES = 5


def _project(x, sh_ref, sc_ref, g1_ref, w_ref, gq_ref, gk_ref, cos_ref, sin_ref, bd_ref,
             q_ref, kt_ref, v_ref, zc_ref, zs_ref, rope, between=None):
    hook = between if between is not None else (lambda phase: None)
    hook(0)
    h = _modulated(x, sh_ref, sc_ref, g1_ref).astype(BF16)
    hook(1)
    q = _qk_head(_dot(h, w_ref[:, Q0:K0]), gq_ref[...], bd_ref[...], cos_ref, sin_ref, rope)
    q_ref[...] = (q * (HEAD_DIM ** -0.5 * LOG2_E)).astype(BF16)
    hook(2)
    kv = _dot(h, w_ref[:, K0:CV0])
    k = _qk_head(kv[:, 0:KV_W], gk_ref[...], bd_ref[0:KV_W, 0:KV_W], cos_ref, sin_ref, rope)
    kt_ref[...] = k.T.astype(BF16)
    v_ref[...] = kv[:, KV_W:].astype(BF16)
    hook(3)
    zc_ref[...] = _dot(h, w_ref[:, CV0:CH0])
    hook(4)
    zs_ref[...] = _dot(h, w_ref[:, CH0:IN_W])


def _proj_kernel(x_ref, *refs, rope):
    _project(x_ref[...], *refs, rope)


def _proj_kv_kernel(x_ref, sh_ref, sc_ref, g1_ref, w_ref, gk_ref, bd_ref, kt_ref, v_ref):
    h = _modulated(x_ref[...], sh_ref, sc_ref, g1_ref)
    p = _dot(h.astype(BF16), w_ref[...])
    k = _qk_head(p[:, 0:KV_W], gk_ref[...], bd_ref[0:KV_W, 0:KV_W], None, None, False)
    kt_ref[...] = k.T.astype(BF16)
    v_ref[...] = p[:, KV_W:2 * KV_W].astype(BF16)


def _mod_spec(layer, row_of, chunk):
    return pl.BlockSpec((None, None, 1, D_MODEL), lambda b, j: (layer, row_of(b), 0, chunk))


def _const_spec(shape):
    return pl.BlockSpec(shape, lambda b, j: tuple(0 for _ in shape))


def _proj(x, mod4, layer, row_of, g1, w_in_b, gq, gk, cos2, sin2, bd, *, rope, tm):
    bsz, s, d = x.shape
    grid = (bsz, s // tm)
    row = lambda w: pl.BlockSpec((None, tm, w), lambda b, j: (b, j, 0))
    return pl.pallas_call(
        functools.partial(_proj_kernel, rope=rope),
        out_shape=(
            jax.ShapeDtypeStruct((bsz, s, ATTN_W), BF16),
            jax.ShapeDtypeStruct((bsz, KV_W, s), BF16),
            jax.ShapeDtypeStruct((bsz, s, KV_W), BF16),
            jax.ShapeDtypeStruct((bsz, s, 2 * CONV_W), F32),
            jax.ShapeDtypeStruct((bsz, s, 2 * CHUNK_W), F32),
        ),
        grid=grid,
        in_specs=[
            row(d),
            _mod_spec(layer, row_of, 0),
            _mod_spec(layer, row_of, 1),
            _const_spec((1, d)),
            _const_spec((d, IN_W)),
            _const_spec((1, ATTN_W)),
            _const_spec((1, KV_W)),
            pl.BlockSpec((tm, LANES), lambda b, j: (j, 0)),
            pl.BlockSpec((tm, LANES), lambda b, j: (j, 0)),
            _const_spec((ATTN_W, ATTN_W)),
        ],
        out_specs=(
            row(ATTN_W),
            pl.BlockSpec((None, KV_W, tm), lambda b, j: (b, 0, j)),
            row(KV_W),
            row(2 * CONV_W),
            row(2 * CHUNK_W),
        ),
        compiler_params=_cparams("arbitrary", "arbitrary"),
        name="proj",
    )(x, mod4, mod4, g1, w_in_b, gq, gk, cos2, sin2, bd)


def _proj_kv(x, mod4, layer, row_of, g1, w_kv_b, gk, bd, *, tm):
    bsz, s, d = x.shape
    return pl.pallas_call(
        _proj_kv_kernel,
        out_shape=(
            jax.ShapeDtypeStruct((bsz, KV_W, s), BF16),
            jax.ShapeDtypeStruct((bsz, s, KV_W), BF16),
        ),
        grid=(bsz, s // tm),
        in_specs=[
            pl.BlockSpec((None, tm, d), lambda b, j: (b, j, 0)),
            _mod_spec(layer, row_of, 0),
            _mod_spec(layer, row_of, 1),
            _const_spec((1, d)),
            _const_spec((d, 2 * KV_W)),
            _const_spec((1, KV_W)),
            _const_spec((ATTN_W, ATTN_W)),
        ],
        out_specs=(
            pl.BlockSpec((None, KV_W, tm), lambda b, j: (b, 0, j)),
            pl.BlockSpec((None, tm, KV_W), lambda b, j: (b, j, 0)),
        ),
        compiler_params=_cparams("arbitrary", "arbitrary"),
        name="proj_kv",
    )(x, mod4, mod4, g1, w_kv_b, gk, bd)


def _attn_kernel(*refs, n_src):
    q_ref = refs[0]
    kt_refs = refs[1:1 + n_src]
    v_refs = refs[1 + n_src:1 + 2 * n_src]
    o_ref, s_ref, p_ref = refs[1 + 2 * n_src:]
    tiles, spans, col = [], [], 0
    for i, kt in enumerate(kt_refs):
        kn = kt.shape[1]
        tk = next(w for w in (ATTN_K_TILE, MXU_COLS, LANES) if kn % w == 0)
        tiles += [(i, o, col + o, tk) for o in range(0, kn, tk)]
        spans.append((col, kn))
        col += kn

    def scores_into(slot, h):
        g = h // Q_GROUP
        qh = q_ref[:, h * HEAD_DIM:(h + 1) * HEAD_DIM]
        mx = None
        for i, o, c, tk in tiles:
            tile = _dot(qh, kt_refs[i][g * HEAD_DIM:(g + 1) * HEAD_DIM, o:o + tk])
            s_ref[slot, :, c:c + tk] = tile
            for lo in range(0, tk, LANES):
                mx = tile[:, lo:lo + LANES] if mx is None else jnp.maximum(mx, tile[:, lo:lo + LANES])
        return jnp.max(mx, axis=-1, keepdims=True)

    m_next = scores_into(0, 0)
    outs = []
    for h in range(N_Q_HEADS):
        g, slot, m = h // Q_GROUP, h % 2, m_next
        if h + 1 < N_Q_HEADS:
            m_next = scores_into(1 - slot, h + 1)
        p = jnp.exp2(s_ref[slot] - m)
        denom = jnp.sum(p, axis=-1, keepdims=True)
        p_ref[slot] = p.astype(BF16)
        o = functools.reduce(jnp.add, [_dot(p_ref[slot, :, c:c + kn], v[...]) for (c, kn), v in zip(spans, v_refs)])
        outs.append(o[:, g * HEAD_DIM:(g + 1) * HEAD_DIM] / denom)
    o_ref[...] = jnp.concatenate(outs, axis=1).astype(BF16)


def _attention(q, kts, vs, *, tq):
    bsz, nq, _ = q.shape
    n_src = len(kts)
    kn_all = sum(kt.shape[2] for kt in kts)
    in_specs = [pl.BlockSpec((None, tq, ATTN_W), lambda b, j: (b, j, 0))]
    in_specs += [pl.BlockSpec((None, KV_W, kt.shape[2]), lambda b, j: (b, 0, 0)) for kt in kts]
    in_specs += [pl.BlockSpec((None, v.shape[1], KV_W), lambda b, j: (b, 0, 0)) for v in vs]
    return pl.pallas_call(
        functools.partial(_attn_kernel, n_src=n_src),
        out_shape=jax.ShapeDtypeStruct((bsz, nq, ATTN_W), BF16),
        grid=(bsz, nq // tq),
        in_specs=in_specs,
        out_specs=pl.BlockSpec((None, tq, ATTN_W), lambda b, j: (b, j, 0)),
        scratch_shapes=[pltpu.VMEM((2, tq, kn_all), F32), pltpu.VMEM((2, tq, kn_all), BF16)],
        compiler_params=_cparams("arbitrary", "arbitrary"),
        name="attention",
    )(q, *kts, *vs)


def _glu(z):
    return z[:, :CONV_W] * jax.nn.sigmoid(z[:, CONV_W:])


def _conv_kernel(zp_ref, z_ref, zn_ref, w_ref, bdw_ref, g_ref, b_ref, o_ref, u_ref, us_ref, *, ts):
    j = pl.program_id(1)
    nj = pl.num_programs(1)
    u_ref[0:CONV_HALO, :] = jnp.where(j > 0, _glu(zp_ref[...]), 0.0)
    u_ref[CONV_HALO + ts:CONV_HALO + ts + CONV_HALO, :] = jnp.where(j < nj - 1, _glu(zn_ref[...]), 0.0)
    u_ref[CONV_HALO:CONV_HALO + ts, :] = _glu(z_ref[...])
    rows = us_ref.shape[1]
    for b in range(1, SUBLANES):
        us_ref[b - 1] = u_ref[b:b + rows, :]
    off = CONV_HALO - CONV_K // 2
    for c in range(ts // CONV_ROWS):
        acc = jnp.zeros((CONV_ROWS, CONV_W), F32)
        for k in range(CONV_K):
            start = c * CONV_ROWS + off + k
            b, lo = start % SUBLANES, start - start % SUBLANES
            window = u_ref[lo:lo + CONV_ROWS, :] if b == 0 else us_ref[b - 1, lo:lo + CONV_ROWS, :]
            acc = acc + window * w_ref[k:k + 1, :]
        y = _layernorm(acc + bdw_ref[...], g_ref[...], b_ref[...])
        o_ref[c * CONV_ROWS:(c + 1) * CONV_ROWS, :] = (y * jax.nn.sigmoid(y)).astype(BF16)


def _conv(zc, w_dw, b_dw, g_ln, b_ln, *, ts):
    bsz, s, _ = zc.shape
    nh = ts // CONV_HALO
    last = s // CONV_HALO - 1
    return pl.pallas_call(
        functools.partial(_conv_kernel, ts=ts),
        out_shape=jax.ShapeDtypeStruct((bsz, s, CONV_W), BF16),
        grid=(bsz, s // ts),
        in_specs=[
            pl.BlockSpec((None, CONV_HALO, 2 * CONV_W), lambda b, j: (b, jnp.maximum(j * nh - 1, 0), 0)),
            pl.BlockSpec((None, ts, 2 * CONV_W), lambda b, j: (b, j, 0)),
            pl.BlockSpec((None, CONV_HALO, 2 * CONV_W), lambda b, j: (b, jnp.minimum((j + 1) * nh, last), 0)),
            _const_spec((CONV_K, CONV_W)),
            _const_spec((1, CONV_W)),
            _const_spec((1, CONV_W)),
            _const_spec((1, CONV_W)),
        ],
        out_specs=pl.BlockSpec((None, ts, CONV_W), lambda b, j: (b, j, 0)),
        scratch_shapes=[pltpu.VMEM((ts + 2 * CONV_HALO, CONV_W), F32),
                        pltpu.VMEM((SUBLANES - 1, ts + 2 * CONV_HALO - SUBLANES, CONV_W), F32)],
        compiler_params=_cparams("arbitrary", "arbitrary"),
        name="conv",
    )(zc, zc, zc, w_dw, b_dw, g_ln, b_ln)


def _sgu_kernel(z_ref, g_ref, b_ref, ws_ref, bs_ref, o_ref, *, tm):
    z = z_ref[...]
    z = 0.5 * z * (1.0 + lax.erf(z * (2.0 ** -0.5)))
    u = z[:, :CHUNK_W]
    v = _layernorm(z[:, CHUNK_W:], g_ref[...], b_ref[...]).astype(BF16)
    head = lax.broadcasted_iota(I32, (CHUNK, CHUNK_W), 1) // (CHUNK_W // CHUNK_HEADS)
    for c in range(tm // CHUNK):
        vc = v[c * CHUNK:(c + 1) * CHUNK, :]
        s = bs_ref[...]
        for h in range(CHUNK_HEADS):
            s = s + jnp.where(head == h, _dot(ws_ref[h], vc), 0.0)
        o_ref[c * CHUNK:(c + 1) * CHUNK, :] = (u[c * CHUNK:(c + 1) * CHUNK, :] * s).astype(BF16)


def _sgu(zs, g_ln, b_ln, ws_b, bs_full, *, tm):
    bsz, s, _ = zs.shape
    return pl.pallas_call(
        functools.partial(_sgu_kernel, tm=tm),
        out_shape=jax.ShapeDtypeStruct((bsz, s, CHUNK_W), BF16),
        grid=(bsz, s // tm),
        in_specs=[
            pl.BlockSpec((None, tm, 2 * CHUNK_W), lambda b, j: (b, j, 0)),
            _const_spec((1, CHUNK_W)),
            _const_spec((1, CHUNK_W)),
            _const_spec((CHUNK_HEADS, CHUNK, CHUNK)),
            _const_spec((CHUNK, CHUNK_W)),
        ],
        out_specs=pl.BlockSpec((None, tm, CHUNK_W), lambda b, j: (b, j, 0)),
        compiler_params=_cparams("arbitrary", "arbitrary"),
        name="sgu",
    )(zs, g_ln, b_ln, ws_b, bs_full)


def _oproj_kernel(a_ref, cv_ref, ch_ref, x_ref, gt_ref, sh_ref, sc_ref, g2_ref, wo_ref, xo_ref, h2_ref):
    y = (_dot(a_ref[...], wo_ref[0:ATTN_W, :])
         + _dot(cv_ref[...], wo_ref[ATTN_W:ATTN_W + CONV_W, :])
         + _dot(ch_ref[...], wo_ref[ATTN_W + CONV_W:, :]))
    xn = x_ref[...] + gt_ref[...] * y
    xo_ref[...] = xn
    h2_ref[...] = _rmsnorm(xn, g2_ref[...]) * (1.0 + sc_ref[...]) + sh_ref[...]


def _oproj(a, cv, ch, x, mod4, layer, row_of, g2, wo_b, *, tm):
    bsz, s, d = x.shape
    row = lambda w: pl.BlockSpec((None, tm, w), lambda b, j: (b, j, 0))
    return pl.pallas_call(
        _oproj_kernel,
        out_shape=(jax.ShapeDtypeStruct((bsz, s, d), F32), jax.ShapeDtypeStruct((bsz, s, d), F32)),
        grid=(bsz, s // tm),
        in_specs=[
            row(ATTN_W), row(CONV_W), row(CHUNK_W), row(d),
            _mod_spec(layer, row_of, 2),
            _mod_spec(layer, row_of, 3),
            _mod_spec(layer, row_of, 4),
            _const_spec((1, d)),
            _const_spec((d, d)),
        ],
        out_specs=(row(d), row(d)),
        compiler_params=_cparams("arbitrary", "arbitrary"),
        name="oproj",
    )(a, cv, ch, x, mod4, mod4, mod4, g2, wo_b)


def _router_kernel(*refs, tb, first_blocks):
    n_src = 1 if first_blocks is None else 2
    h_refs = refs[:n_src]
    wr_ref, br_ref, idx_ref, wt_ref, rank_ref, cnt_ref, carry_ref = refs[n_src:]

    @pl.when(pl.program_id(0) == 0)
    def _():
        carry_ref[...] = jnp.zeros_like(carry_ref)

    h = h_refs[0][...] if n_src == 1 else jnp.where(pl.program_id(0) < first_blocks, h_refs[0][...], h_refs[1][...])
    logits = _dot3(wr_ref[...], h, dot=_dot_nt) + br_ref[...]
    e_iota = lax.broadcasted_iota(I32, (N_EXPERTS, tb), 0)
    vals, idxs = [], []
    work = logits
    for _ in range(TOP_K):
        m = jnp.max(work, axis=0, keepdims=True)
        ik = jnp.min(jnp.where(work == m, e_iota, N_EXPERTS), axis=0, keepdims=True)
        vals.append(m)
        idxs.append(ik)
        work = jnp.where(e_iota == ik, -jnp.inf, work)
    exps = [jnp.exp(v - vals[0]) for v in vals]
    denom = functools.reduce(jnp.add, exps)
    wt_ref[...] = jnp.concatenate([e / denom for e in exps], axis=0)
    idx_ref[...] = jnp.concatenate(idxs, axis=0)

    onehot = functools.reduce(jnp.add, [(e_iota == ik).astype(F32) for ik in idxs])
    upper = (lax.broadcasted_iota(I32, (tb, tb), 0) < lax.broadcasted_iota(I32, (tb, tb), 1)).astype(BF16)
    before = _dot(onehot.astype(BF16), upper) + carry_ref[...]
    ranks = [jnp.sum(jnp.where(e_iota == ik, before, 0.0), axis=0, keepdims=True) for ik in idxs]
    rank_ref[...] = jnp.concatenate(ranks, axis=0).astype(I32)
    carry_ref[...] = carry_ref[...] + jnp.sum(onehot, axis=1, keepdims=True)
    cnt_ref[...] = jnp.broadcast_to(carry_ref[...], cnt_ref.shape)


def _source_specs(hs, tb):
    d = hs[0].shape[1]
    if len(hs) == 1:
        return [pl.BlockSpec((tb, d), lambda i, *_: (i, 0))], None
    nl = hs[0].shape[0] // tb
    return [pl.BlockSpec((tb, d), lambda i, *_: (jnp.minimum(i, nl - 1), 0)),
            pl.BlockSpec((tb, d), lambda i, *_: (jnp.maximum(i - nl, 0), 0))], nl


def _router(hs, wr_t, br, *, tb):
    n, d = sum(h.shape[0] for h in hs), hs[0].shape[1]
    tok = lambda dt: jax.ShapeDtypeStruct((TOP_K, n), dt)
    tok_spec = pl.BlockSpec((TOP_K, tb), lambda i: (0, i))
    src_specs, first_blocks = _source_specs(hs, tb)
    return pl.pallas_call(
        functools.partial(_router_kernel, tb=tb, first_blocks=first_blocks),
        out_shape=(tok(I32), tok(F32), tok(I32), jax.ShapeDtypeStruct((N_EXPERTS, LANES), F32)),
        grid=(n // tb,),
        in_specs=src_specs + [
            pl.BlockSpec((N_EXPERTS, d), lambda i: (0, 0)),
            pl.BlockSpec((N_EXPERTS, 1), lambda i: (0, 0)),
        ],
        out_specs=(tok_spec, tok_spec, tok_spec, pl.BlockSpec((N_EXPERTS, LANES), lambda i: (0, 0))),
        scratch_shapes=[pltpu.VMEM((N_EXPERTS, 1), F32)],
        compiler_params=_cparams("arbitrary"),
        name="router",
    )(*hs, wr_t, br)


def _pos_kernel(start_ref, idx_ref, rank_ref, pos_ref):
    idx = idx_ref[...]
    pos = rank_ref[...]
    for e in range(N_EXPERTS):
        pos = pos + jnp.where(idx == e, start_ref[e], 0)
    pos_ref[...] = pos


def _positions(starts, idx, rank):
    return pl.pallas_call(
        _pos_kernel,
        out_shape=jax.ShapeDtypeStruct(idx.shape, I32),
        grid_spec=pltpu.PrefetchScalarGridSpec(
            num_scalar_prefetch=1, grid=(1,),
            in_specs=[pl.BlockSpec(idx.shape, lambda i, s: (0, 0)), pl.BlockSpec(idx.shape, lambda i, s: (0, 0))],
            out_specs=pl.BlockSpec(idx.shape, lambda i, s: (0, 0))),
        compiler_params=_cparams("arbitrary"),
        name="positions",
    )(starts, idx, rank)


def _row_copy(src, src_row, dst, dst_row, sem):
    return pltpu.make_async_copy(src.at[pl.ds(src_row, 1)], dst.at[pl.ds(dst_row, 1)], sem)


def _dispatch_kernel(zf_ref, pos_ref, *refs, tb, tm, nblk, first_blocks):
    n_src = 1 if first_blocks is None else 2
    h_refs = refs[:n_src]
    xs_ref, zbuf_ref, sem, zsem = refs[n_src:]

    @pl.when(pl.program_id(0) == 0)
    def _():
        zbuf_ref[...] = jnp.zeros_like(zbuf_ref)

        def zero_copy(i):
            return pltpu.make_async_copy(zbuf_ref, xs_ref.at[pl.ds(pl.multiple_of(i * tm, tm), tm)], zsem)

        def start(i, carry):
            @pl.when(zf_ref[i] != 0)
            def _():
                zero_copy(i).start()
            return carry

        def wait(i, carry):
            @pl.when(zf_ref[i] != 0)
            def _():
                zero_copy(i).wait()
            return carry

        lax.fori_loop(0, nblk, start, 0)
        lax.fori_loop(0, nblk, wait, 0)

    def scatter_rows(h_ref):
        def issue(t8, carry):
            t0 = pl.multiple_of(t8 * SUBLANES, SUBLANES)
            for r in range(SUBLANES):
                for k in range(TOP_K):
                    _row_copy(h_ref, t0 + r, xs_ref, pos_ref[0, k * tb + t0 + r], sem).start(priority=(r + k) % 2)
            return carry

        lax.fori_loop(0, tb // SUBLANES, issue, 0)

    if n_src == 1:
        scatter_rows(h_refs[0])
    else:
        pl.when(pl.program_id(0) < first_blocks)(lambda: scatter_rows(h_refs[0]))
        pl.when(pl.program_id(0) >= first_blocks)(lambda: scatter_rows(h_refs[1]))
    pltpu.make_async_copy(xs_ref.at[pl.ds(0, TOP_K * tb)], xs_ref.at[pl.ds(0, TOP_K * tb)], sem).wait()


def _dispatch(zero_flag, pos_blocks, hs, *, tb, tm):
    n, d = sum(h.shape[0] for h in hs), hs[0].shape[1]
    nblk = zero_flag.shape[0]
    src_specs, first_blocks = _source_specs(hs, tb)
    return pl.pallas_call(
        functools.partial(_dispatch_kernel, tb=tb, tm=tm, nblk=nblk, first_blocks=first_blocks),
        out_shape=jax.ShapeDtypeStruct((nblk * tm, d), F32),
        grid_spec=pltpu.PrefetchScalarGridSpec(
            num_scalar_prefetch=1, grid=(n // tb,),
            in_specs=[pl.BlockSpec((None, 1, TOP_K * tb), lambda i, zf: (i, 0, 0), memory_space=pltpu.SMEM)]
            + src_specs,
            out_specs=pl.BlockSpec(memory_space=pl.ANY),
            scratch_shapes=[pltpu.VMEM((tm, d), F32), pltpu.SemaphoreType.DMA(()), pltpu.SemaphoreType.DMA(())]),
        compiler_params=_cparams("arbitrary"),
        name="dispatch",
    )(zero_flag, pos_blocks, *hs)


def _expert_kernel(be_ref, ne_ref, nb_ref, xs_ref, wg_ref, bg_ref, wu_ref, bu_ref, wd_ref, bd_ref, ys_ref,
                   wf_ref, wb_ref, sems, *, layer):
    i = pl.program_id(0)
    used = i < nb_ref[0]
    new_expert = jnp.logical_or(i == 0, be_ref[i] != be_ref[jnp.maximum(i - 1, 0)])

    def fetch(e):
        return [pltpu.make_async_copy(w.at[layer, e], wf_ref.at[m], sems.at[m])
                for m, w in enumerate((wg_ref, wu_ref, wd_ref))]

    @pl.when(i == 0)
    def _():
        for cp in fetch(be_ref[0]):
            cp.start()

    @pl.when(jnp.logical_and(used, new_expert))
    def _():
        for m, cp in enumerate(fetch(be_ref[i])):
            cp.wait()
            wb_ref[m] = wf_ref[m].astype(BF16)
        nxt = ne_ref[i]

        @pl.when(nxt >= 0)
        def _():
            for cp in fetch(nxt):
                cp.start()

    @pl.when(used)
    def _():
        x = xs_ref[...].astype(BF16)
        g = jnp.minimum(_dot(x, wb_ref[0]) + bg_ref[...], SWIGLU_LIMIT)
        u = jnp.clip(_dot(x, wb_ref[1]) + bu_ref[...], -SWIGLU_LIMIT, SWIGLU_LIMIT)
        a = (u + 1.0) * (g * jax.nn.sigmoid(SWIGLU_ALPHA * g))
        ys_ref[...] = _dot(a.astype(BF16), wb_ref[2]) + bd_ref[...]

    @pl.when(jnp.logical_not(used))
    def _():
        ys_ref[...] = jnp.zeros_like(ys_ref)


def _experts(block_e, next_e, n_used, xs, layer, wg, bg, wu, bu, wd, bd, *, tm):
    p, d = xs.shape
    assert wg.shape[2:] == (d, d) and wd.shape[2:] == (d, d)
    bias = pl.BlockSpec((None, None, 1, d), lambda i, be, ne, nb: (layer, be[i], 0, 0))
    hbm = pl.BlockSpec(memory_space=pl.ANY)
    return pl.pallas_call(
        functools.partial(_expert_kernel, layer=layer),
        out_shape=jax.ShapeDtypeStruct((p, d), F32),
        grid_spec=pltpu.PrefetchScalarGridSpec(
            num_scalar_prefetch=3, grid=(p // tm,),
            in_specs=[pl.BlockSpec((tm, d), lambda i, be, ne, nb: (i, 0)), hbm, bias, hbm, bias, hbm, bias],
            out_specs=pl.BlockSpec((tm, d), lambda i, be, ne, nb: (i, 0)),
            scratch_shapes=[pltpu.VMEM((3, d, d), F32), pltpu.VMEM((3, d, d), BF16), pltpu.SemaphoreType.DMA((3,))]),
        compiler_params=_cparams("arbitrary"),
        name="experts",
    )(block_e, next_e, n_used, xs, wg, bg, wu, bu, wd, bd)


def _combine_kernel(pos_ref, posn_ref, ys_ref, wt_ref, x_ref, gt_ref, gf_ref, o_ref, buf_ref, sems, *, tb, final_norm):
    i = pl.program_id(0)
    n = pl.num_programs(0)

    def gather(p_ref, slot):
        def issue(t8, carry):
            t0 = pl.multiple_of(t8 * SUBLANES, SUBLANES)
            for r in range(SUBLANES):
                for k in range(TOP_K):
                    _row_copy(ys_ref, p_ref[0, k * tb + t0 + r], buf_ref.at[slot, k], t0 + r,
                              sems.at[slot]).start(priority=(r + k) % 2)
            return carry

        lax.fori_loop(0, tb // SUBLANES, issue, 0)

    @pl.when(i == 0)
    def _():
        gather(pos_ref, 0)

    @pl.when(i + 1 < n)
    def _():
        gather(posn_ref, (i + 1) % 2)

    slot = i % 2
    for k in range(TOP_K):
        pltpu.make_async_copy(ys_ref.at[pl.ds(0, tb)], buf_ref.at[slot, k], sems.at[slot]).wait()
    m = functools.reduce(jnp.add, [buf_ref[slot, k] * wt_ref[:, k:k + 1] for k in range(TOP_K)])
    xn = x_ref[...] + gt_ref[...] * m
    o_ref[...] = _rmsnorm(xn, gf_ref[...]) if final_norm else xn


def _combine(pos_blocks, ys, wt_tok, x, mod4, layer, row_of, g_final, blk0, *, tb, final_norm):
    bsz, s, d = x.shape
    nj = s // tb
    steps = bsz * nj
    return pl.pallas_call(
        functools.partial(_combine_kernel, tb=tb, final_norm=final_norm),
        out_shape=jax.ShapeDtypeStruct((bsz, s, d), F32),
        grid=(steps,),
        in_specs=[
            pl.BlockSpec((None, 1, TOP_K * tb), lambda i: (blk0 + i, 0, 0), memory_space=pltpu.SMEM),
            pl.BlockSpec((None, 1, TOP_K * tb), lambda i: (blk0 + jnp.minimum(i + 1, steps - 1), 0, 0),
                         memory_space=pltpu.SMEM),
            pl.BlockSpec(memory_space=pl.ANY),
            pl.BlockSpec((tb, TOP_K), lambda i: (blk0 + i, 0)),
            pl.BlockSpec((None, tb, d), lambda i: (i // nj, i % nj, 0)),
            pl.BlockSpec((None, None, 1, d), lambda i: (layer, row_of(i // nj), 0, 5)),
            pl.BlockSpec((1, d), lambda i: (0, 0)),
        ],
        out_specs=pl.BlockSpec((None, tb, d), lambda i: (i // nj, i % nj, 0)),
        scratch_shapes=[pltpu.VMEM((2, TOP_K, tb, d), F32), pltpu.SemaphoreType.DMA((2,))],
        compiler_params=_cparams("arbitrary"),
        name="combine",
    )(pos_blocks, pos_blocks, ys, wt_tok, x, mod4, g_final)


def _combine_proj_kernel(pos_ref, posn_ref, ys_ref, wt_ref, x_ref, gt_ref, *refs, tb):
    proj_refs, xo_ref, out_refs, buf_ref, sems = refs[:9], refs[9], refs[10:15], refs[15], refs[16]
    i = pl.program_id(0)
    n = pl.num_programs(0)

    def drain(slot):
        for k in range(TOP_K):
            pltpu.make_async_copy(ys_ref.at[pl.ds(0, tb)], buf_ref.at[slot, k], sems.at[slot]).wait()

    @pl.when(i == 0)
    def _():
        def issue(t, carry):
            for k in range(TOP_K):
                _row_copy(ys_ref, pos_ref[0, k * tb + t], buf_ref.at[0, k], t, sems.at[0]).start(priority=k % 2)
            return carry

        lax.fori_loop(0, tb, issue, 0)

    slot = i % 2
    drain(slot)
    m = functools.reduce(jnp.add, [buf_ref[slot, k] * wt_ref[:, k:k + 1] for k in range(TOP_K)])
    xn = x_ref[...] + gt_ref[...] * m
    xo_ref[...] = xn

    def gather_slice(phase):
        for t in range(phase * tb // PROJ_PHASES, (phase + 1) * tb // PROJ_PHASES):
            for k in range(TOP_K):
                _row_copy(ys_ref, posn_ref[0, k * tb + t], buf_ref.at[1 - slot, k], t,
                          sems.at[1 - slot]).start(priority=(t + k) % 2)

    _project(xn, *proj_refs, *out_refs, True, between=gather_slice)

    @pl.when(i == n - 1)
    def _():
        drain(1 - slot)


def _combine_proj(pos_blocks, ys, wt_tok, x, mod4, prev_layer, layer, g1, w_in_b, gq, gk, cos2, sin2, bd, *, tm):
    bsz, s, d = x.shape
    nj = s // tm
    steps = bsz * nj
    row = lambda w: pl.BlockSpec((None, tm, w), lambda i: (i // nj, i % nj, 0))
    mod = lambda l, c: pl.BlockSpec((None, None, 1, d), lambda i: (l, i // nj, 0, c))
    const = lambda shape: pl.BlockSpec(shape, lambda i: tuple(0 for _ in shape))
    smem = lambda f: pl.BlockSpec((None, 1, TOP_K * tm), lambda i: (f(i), 0, 0), memory_space=pltpu.SMEM)
    return pl.pallas_call(
        functools.partial(_combine_proj_kernel, tb=tm),
        out_shape=(
            jax.ShapeDtypeStruct((bsz, s, d), F32),
            jax.ShapeDtypeStruct((bsz, s, ATTN_W), BF16),
            jax.ShapeDtypeStruct((bsz, KV_W, s), BF16),
            jax.ShapeDtypeStruct((bsz, s, KV_W), BF16),
            jax.ShapeDtypeStruct((bsz, s, 2 * CONV_W), F32),
            jax.ShapeDtypeStruct((bsz, s, 2 * CHUNK_W), F32),
        ),
        grid=(steps,),
        in_specs=[
            smem(lambda i: i),
            smem(lambda i: jnp.minimum(i + 1, steps - 1)),
            pl.BlockSpec(memory_space=pl.ANY),
            pl.BlockSpec((tm, TOP_K), lambda i: (i, 0)),
            row(d),
            mod(prev_layer, 5),
            mod(layer, 0),
            mod(layer, 1),
            const((1, d)),
            const((d, IN_W)),
            const((1, ATTN_W)),
            const((1, KV_W)),
            pl.BlockSpec((tm, LANES), lambda i: (i % nj, 0)),
            pl.BlockSpec((tm, LANES), lambda i: (i % nj, 0)),
            const((ATTN_W, ATTN_W)),
        ],
        out_specs=(
            row(d),
            row(ATTN_W),
            pl.BlockSpec((None, KV_W, tm), lambda i: (i // nj, 0, i % nj)),
            row(KV_W),
            row(2 * CONV_W),
            row(2 * CHUNK_W),
        ),
        scratch_shapes=[pltpu.VMEM((2, TOP_K, tm, d), F32), pltpu.SemaphoreType.DMA((2,))],
        compiler_params=_cparams("arbitrary"),
        name="combine_proj",
    )(pos_blocks, pos_blocks, ys, wt_tok, x, mod4, mod4, mod4, g1, w_in_b, gq, gk, cos2, sin2, bd)


def _block_layout(pos, tb):
    n = pos.shape[1]
    return pos.reshape(TOP_K, n // tb, tb).transpose(1, 0, 2).reshape(n // tb, 1, TOP_K * tb)


def _moe(hs, wr_t, br, layer, wg, bg, wu, bu, wd, bd):
    n = sum(h.shape[0] for h in hs)
    tm = EXPERT_TILE
    common = functools.reduce(math.gcd, [h.shape[0] for h in hs])
    td = _fit(common, DISPATCH_TILE)
    idx, wt, rank, cnt = _router(hs, wr_t, br, tb=_fit(common, ROUTER_TILE))
    counts = cnt[:, 0].astype(I32)
    padded = (counts + tm - 1) // tm * tm
    ends = jnp.cumsum(padded)
    pos = _positions((ends - padded).astype(I32), idx, rank)
    nblk = (n * TOP_K) // tm + N_EXPERTS
    n_used = ends[-1] // tm
    blk = jnp.arange(nblk, dtype=I32)
    first_row = jnp.minimum(blk, n_used - 1) * tm
    block_e = jnp.minimum(jnp.sum(ends[None, :] <= first_row[:, None], axis=1), N_EXPERTS - 1).astype(I32)
    partial_or_unused = (blk >= n_used) | jnp.any(((blk + 1) * tm)[:, None] == ends[None, :], axis=1)
    xs = _dispatch(partial_or_unused.astype(I32), _block_layout(pos, td), hs, tb=td, tm=tm)
    e_ids = jnp.arange(N_EXPERTS, dtype=I32)
    later = (e_ids[None, :] > e_ids[:, None]) & (padded[None, :] > 0)
    next_nonempty = jnp.min(jnp.where(later, e_ids[None, :], N_EXPERTS), axis=1)
    next_e = jnp.where(next_nonempty < N_EXPERTS, next_nonempty, -1).astype(I32)[block_e]
    ys = _experts(block_e, next_e, n_used.reshape(1).astype(I32), xs, layer, wg, bg, wu, bu, wd, bd, tm=tm)
    return ys, pos, wt


def _rope_tables(s):
    pos = jnp.arange(s)
    r, col = (pos // GRID_W).astype(F32), (pos % GRID_W).astype(F32)
    inv = ROPE_THETA ** (-jnp.arange(0, AXIS_DIM, 2, dtype=F32) / AXIS_DIM)
    ang = jnp.concatenate([r[:, None] * inv, col[:, None] * inv], axis=-1)
    cos, sin = jnp.cos(ang), jnp.sin(ang)
    return jnp.tile(jnp.concatenate([cos, cos], axis=-1), (1, 2)), jnp.tile(jnp.concatenate([-sin, sin], axis=-1), (1, 2))


def kernel(x, c, ctx, c_ctx, w_ada, b_ada, g_norm1, w_in, g_q, g_k, w_dw, b_dw, g_conv_ln, b_conv_ln,
           g_sgu_ln, b_sgu_ln, w_s, b_s, w_o, g_norm2, w_router, b_router, w_gate, b_gate, w_up, b_up,
           w_down, b_down, g_final):
    bsz, s, d = x.shape
    cl = ctx.shape[1]
    depth = w_ada.shape[0]
    n_lat, n_ctx = bsz * s, bsz * cl
    ctx_row = 8 * ((bsz + 7) // 8)
    rows = ctx_row + 8

    cs = jnp.zeros((rows, d), F32).at[:bsz].set(c).at[ctx_row].set(c_ctx)
    mod4 = _ada(cs, w_ada, b_ada).reshape(depth, rows, 1, 6 * d)
    lat_row = lambda b: b
    ctx_row_of = lambda b: ctx_row

    cos2, sin2 = _rope_tables(s)
    cosc = jnp.zeros((cl, LANES), F32)
    head_of = jnp.arange(ATTN_W) // HEAD_DIM
    bd = (head_of[:, None] == head_of[None, :]).astype(BF16)

    tm_lat = min(ROW_TILE, s)
    tm_ctx = min(ROW_TILE, cl)
    xc = ctx
    out = routed = None
    for l in range(depth):
        last = l == depth - 1
        g1 = g_norm1[l].reshape(1, d)
        w_in_b = w_in[l].astype(BF16)
        gq = jnp.tile(g_q[l], N_Q_HEADS).reshape(1, ATTN_W)
        gk = jnp.tile(g_k[l], N_KV_HEADS).reshape(1, KV_W)
        conv_args = (w_dw[l], b_dw[l].reshape(1, -1), g_conv_ln[l].reshape(1, -1), b_conv_ln[l].reshape(1, -1))
        sgu_args = (g_sgu_ln[l].reshape(1, -1), b_sgu_ln[l].reshape(1, -1), w_s[l].astype(BF16),
                    jnp.repeat(b_s[l].T, CHUNK_W // CHUNK_HEADS, axis=1))
        wo_b = w_o[l].astype(BF16)
        g2 = g_norm2[l].reshape(1, d)

        if routed is None:
            q, kt, v, zc, zs = _proj(x, mod4, l, lat_row, g1, w_in_b, gq, gk, cos2, sin2, bd, rope=True, tm=tm_lat)
        else:
            x, q, kt, v, zc, zs = _combine_proj(*routed, x, mod4, l - 1, l, g1, w_in_b, gq, gk, cos2, sin2, bd,
                                                tm=tm_lat)
        if last:
            ktc, vc = _proj_kv(xc, mod4, l, ctx_row_of, g1, w_in_b[:, K0:CV0], gk, bd, tm=tm_ctx)
        else:
            qc, ktc, vc, zcc, zsc = _proj(xc, mod4, l, ctx_row_of, g1, w_in_b, gq, gk, cosc, cosc, bd,
                                          rope=False, tm=tm_ctx)
        a = _attention(q, [kt, ktc], [v, vc], tq=min(ATTN_Q_TILE, s))
        cv = _conv(zc, *conv_args, ts=min(CONV_TILE, s))
        ch = _sgu(zs, *sgu_args, tm=tm_lat)
        x, h2 = _oproj(a, cv, ch, x, mod4, l, lat_row, g2, wo_b, tm=tm_lat)
        tokens = [h2.reshape(n_lat, d)]
        if not last:
            ac = _attention(qc, [ktc], [vc], tq=min(ATTN_Q_TILE, cl))
            cvc = _conv(zcc, *conv_args, ts=min(CONV_TILE, cl))
            chc = _sgu(zsc, *sgu_args, tm=tm_ctx)
            xc, h2c = _oproj(ac, cvc, chc, xc, mod4, l, ctx_row_of, g2, wo_b, tm=tm_ctx)
            tokens.append(h2c.reshape(n_ctx, d))

        ys, pos, wt = _moe(tokens, w_router[l].T, b_router[l].reshape(N_EXPERTS, 1), l,
                           w_gate, b_gate.reshape(depth, N_EXPERTS, 1, -1),
                           w_up, b_up.reshape(depth, N_EXPERTS, 1, -1),
                           w_down, b_down.reshape(depth, N_EXPERTS, 1, -1))
        tb = _fit(s, FINAL_COMBINE_TILE) if last else _fit(cl, COMBINE_TILE)
        pos_blocks = _block_layout(pos, tb)
        wt_tok = wt.T
        gf = g_final.reshape(1, d)
        if last:
            out = _combine(pos_blocks, ys, wt_tok, x, mod4, l, lat_row, gf, 0, tb=tb, final_norm=True)
        else:
            xc = _combine(pos_blocks, ys, wt_tok, xc, mod4, l, ctx_row_of, gf, n_lat // tb, tb=tb, final_norm=False)
            routed = (_block_layout(pos, tm_lat), ys, wt_tok)
    return out
```
